```python
import math
import jax, jax.numpy as jnp
from jax import lax
import numpy as np

D_MODEL = 2048
BATCH = 4
SEQ = 2048
DEPTH = 2

GLA_HEADS = 8
GLA_DK = 64
GLA_DV = 128
GLA_GATE_RANK = 16
GLA_GATE_TEMP = 16.0
GDN_HEADS = 8
GDN_DK = 128
GDN_DV = 128
CONV_WIDTH = 4
CHUNK = 64
SWA_Q_HEADS = 16
SWA_KV_HEADS = 2
SWA_HEAD_DIM = 64
WINDOW = 128
DIFF_HEADS = 8
DIFF_HEAD_DIM = 64
Q_BLOCK = 128
D_FF = 5632
N_EXPERTS = 8
TOP_K = 2
D_EXPERT = 7168
EPS = 1e-6

N_EVEN = (DEPTH + 1) // 2
N_ODD = DEPTH // 2

EVEN_COLS = (GLA_HEADS * GLA_DK, GLA_HEADS * GLA_DK, GLA_HEADS * GLA_DV, GLA_HEADS * GLA_DV, GLA_GATE_RANK,
             GDN_HEADS * GDN_DK, GDN_HEADS * GDN_DK, GDN_HEADS * GDN_DV, GDN_HEADS * GDN_DV, GDN_HEADS, GDN_HEADS)
EVEN_IN = 4 * GLA_HEADS * GLA_DK // 2 + 2 * GLA_HEADS * GLA_DV + GLA_GATE_RANK + 2 * GDN_HEADS * GDN_DK + 2 * GDN_HEADS * GDN_DV + 2 * GDN_HEADS
EVEN_MIX = GLA_HEADS * GLA_DV + GDN_HEADS * GDN_DV
GDN_CONV_CH = 2 * GDN_HEADS * GDN_DK + GDN_HEADS * GDN_DV
ODD_COLS = (SWA_Q_HEADS * SWA_HEAD_DIM, SWA_KV_HEADS * SWA_HEAD_DIM, SWA_KV_HEADS * SWA_HEAD_DIM,
            DIFF_HEADS * 2 * DIFF_HEAD_DIM, DIFF_HEADS * 2 * DIFF_HEAD_DIM, DIFF_HEADS * 2 * DIFF_HEAD_DIM)
ODD_IN = SWA_Q_HEADS * SWA_HEAD_DIM + 2 * SWA_KV_HEADS * SWA_HEAD_DIM + 3 * DIFF_HEADS * 2 * DIFF_HEAD_DIM
ODD_MIX = SWA_Q_HEADS * SWA_HEAD_DIM + DIFF_HEADS * 2 * DIFF_HEAD_DIM

kernel_name = 'hybrid_gla_gdn_swa_diff_moe'


def split_cols(t, sizes):
    idx, acc = [], 0
    for s in sizes[:-1]:
        acc += s
        idx.append(acc)
    return jnp.split(t, idx, axis=-1)


def rmsnorm(x, w):
    xf = x.astype(jnp.float32)
    y = xf * lax.rsqrt(jnp.mean(xf * xf, axis=-1, keepdims=True) + EPS)
    return (y * w.astype(jnp.float32)).astype(x.dtype)


def l2norm(x):
    xf = x.astype(jnp.float32)
    return xf * lax.rsqrt(jnp.sum(xf * xf, axis=-1, keepdims=True) + EPS)


def to_chunks(t):
    B, T, H = t.shape[:3]
    t = t.reshape((B, T // CHUNK, CHUNK, H) + t.shape[3:])
    return t.transpose((1, 0, 3, 2) + tuple(range(4, t.ndim)))


def from_chunks(t):
    n, B, H, C, d = t.shape
    return t.transpose(1, 0, 3, 2, 4).reshape(B, n * C, H, d)


def gla_attention(q, k, v, log_a):
    B, T, H, dk = q.shape
    dv = v.shape[-1]
    qc = to_chunks(q.astype(jnp.float32) * dk ** -0.5)
    kc = to_chunks(k.astype(jnp.float32))
    vc = to_chunks(v.astype(jnp.float32))
    bc = jnp.cumsum(to_chunks(log_a.astype(jnp.float32)), axis=3)
    causal = jnp.tril(jnp.ones((CHUNK, CHUNK), dtype=bool))[:, :, None]

    def step(S, inp):
        qi, ki, vi, bi = inp
        rel = jnp.where(causal, bi[:, :, :, None, :] - bi[:, :, None, :, :], -jnp.inf)
        scores = jnp.einsum('bhik,bhjk,bhijk->bhij', qi, ki, jnp.exp(rel))
        b_end = bi[:, :, -1:, :]
        o = jnp.einsum('bhij,bhjv->bhiv', scores, vi) + jnp.einsum('bhik,bhkv->bhiv', qi * jnp.exp(bi), S)
        S = jnp.exp(b_end[:, :, 0, :, None]) * S + jnp.einsum('bhjk,bhjv->bhkv', ki * jnp.exp(b_end - bi), vi)
        return S, o

    S0 = jnp.zeros((B, H, dk, dv), jnp.float32)
    _, o = lax.scan(step, S0, (qc, kc, vc, bc))
    return from_chunks(o)


def gated_delta_rule(q, k, v, beta, log_g):
    B, T, H, dk = q.shape
    dv = v.shape[-1]
    qc = to_chunks(q.astype(jnp.float32) * dk ** -0.5)
    kc = to_chunks(k.astype(jnp.float32))
    vc = to_chunks(v.astype(jnp.float32))
    bc = to_chunks(beta.astype(jnp.float32))
    gam = jnp.cumsum(to_chunks(log_g.astype(jnp.float32)), axis=-1)
    causal = jnp.tril(jnp.ones((CHUNK, CHUNK), dtype=bool))
    strict = jnp.tril(jnp.ones((CHUNK, CHUNK), dtype=bool), k=-1)
    decay = jnp.exp(jnp.where(causal, gam[..., :, None] - gam[..., None, :], -jnp.inf))
    a_mat = jnp.where(strict, bc[..., :, None] * jnp.einsum('nbhik,nbhjk->nbhij', kc, kc) * decay, 0.0)
    lhs = jnp.eye(CHUNK, dtype=jnp.float32) + a_mat
    w = lax.linalg.triangular_solve(lhs, (bc * jnp.exp(gam))[..., None] * kc, left_side=True, lower=True, unit_diagonal=True)
    u0 = lax.linalg.triangular_solve(lhs, bc[..., None] * vc, left_side=True, lower=True, unit_diagonal=True)
    qk = jnp.einsum('nbhik,nbhjk->nbhij', qc, kc) * decay
    qg = qc * jnp.exp(gam)[..., None]
    kd = kc * jnp.exp(gam[..., -1:] - gam)[..., None]
    g_end = jnp.exp(gam[..., -1])[..., None, None]

    def step(S, inp):
        w_i, u0_i, qk_i, qg_i, kd_i, ge_i = inp
        u = u0_i - jnp.einsum('bhck,bhkv->bhcv', w_i, S)
        o = jnp.einsum('bhck,bhkv->bhcv', qg_i, S) + jnp.einsum('bhcj,bhjv->bhcv', qk_i, u)
        S = ge_i * S + jnp.einsum('bhck,bhcv->bhkv', kd_i, u)
        return S, o

    S0 = jnp.zeros((B, H, dk, dv), jnp.float32)
    _, o = lax.scan(step, S0, (w, u0, qk, qg, kd, g_end))
    return from_chunks(o)


def causal_depthwise_conv(x, w):
    width, ch = w.shape
    return lax.conv_general_dilated(x, w[:, None, :], window_strides=(1,), padding=[(width - 1, 0)],
                                    dimension_numbers=('NWC', 'WIO', 'NWC'), feature_group_count=ch)


def sliding_window_sink_attention(q, k, v, sinks):
    B, T, Hq, d = q.shape
    Hkv = k.shape[2]
    G = Hq // Hkv
    W = WINDOW
    nblk = T // W
    qb = q.astype(jnp.float32).reshape(B, nblk, W, Hkv, G, d)
    kb = k.astype(jnp.float32).reshape(B, nblk, W, Hkv, d)
    vb = v.astype(jnp.float32).reshape(B, nblk, W, Hkv, d)

    def with_prev(t):
        prev = jnp.concatenate([jnp.zeros_like(t[:, :1]), t[:, :-1]], axis=1)
        return jnp.concatenate([prev, t], axis=2)

    kk, vv = with_prev(kb), with_prev(vb)
    s = jnp.einsum('bnqhgd,bnkhd->bnhgqk', qb, kk) * d ** -0.5
    blk = jnp.arange(nblk)[:, None, None]
    qabs = blk * W + jnp.arange(W)[None, :, None]
    kabs = (blk - 1) * W + jnp.arange(2 * W)[None, None, :]
    mask = (kabs >= 0) & (kabs <= qabs) & (kabs > qabs - W)
    s = jnp.where(mask[None, :, None, None], s, -jnp.inf)
    sink = jnp.broadcast_to(sinks.astype(jnp.float32).reshape(1, 1, Hkv, G, 1, 1), s.shape[:-1] + (1,))
    p = jax.nn.softmax(jnp.concatenate([s, sink], axis=-1), axis=-1)[..., :-1]
    o = jnp.einsum('bnhgqk,bnkhd->bnqhgd', p, vv)
    return o.reshape(B, T, Hq * d).astype(q.dtype)


def differential_attention(q, k, v, lam, subln_w, lambda_init):
    B, T, H, _, dh = q.shape
    nblk = T // Q_BLOCK
    qb = q.astype(jnp.float32).reshape(B, nblk, Q_BLOCK, H, 2, dh).transpose(1, 0, 2, 3, 4, 5)
    kf = k.astype(jnp.float32)
    vf = v.astype(jnp.float32)
    kpos = jnp.arange(T)

    def block(args):
        qi, i = args
        s = jnp.einsum('bqhmd,bkhmd->bhmqk', qi, kf) * dh ** -0.5
        qpos = i * Q_BLOCK + jnp.arange(Q_BLOCK)
        s = jnp.where(kpos[None, :] <= qpos[:, None], s, -jnp.inf)
        p = jax.nn.softmax(s, axis=-1)
        pd = p[:, :, 0] - lam * p[:, :, 1]
        return jnp.einsum('bhqk,bkhv->bqhv', pd, vf)

    o = lax.map(block, (qb, jnp.arange(nblk)))
    o = o.transpose(1, 0, 2, 3, 4).reshape(B, T, H, 2 * dh)
    o = rmsnorm(o, subln_w) * (1.0 - lambda_init)
    return o.reshape(B, T, H * 2 * dh).astype(q.dtype)


def even_mixer(x, w_in, w_gate_up, b_gate, gla_norm_w, conv_w, a_log, dt_bias, gdn_norm_w, w_out):
    B, T, _ = x.shape
    qa, ka, va, ga, ra, qb, kb, vb, zb, bb, ab = split_cols(x @ w_in, EVEN_COLS)
    log_a = jax.nn.log_sigmoid((ra @ w_gate_up + b_gate).astype(jnp.float32)) / GLA_GATE_TEMP
    oa = gla_attention(qa.reshape(B, T, GLA_HEADS, GLA_DK), ka.reshape(B, T, GLA_HEADS, GLA_DK),
                       va.reshape(B, T, GLA_HEADS, GLA_DV), log_a.reshape(B, T, GLA_HEADS, GLA_DK))
    oa = rmsnorm(oa, gla_norm_w) * jax.nn.silu(ga.reshape(B, T, GLA_HEADS, GLA_DV).astype(jnp.float32))
    conv = jax.nn.silu(causal_depthwise_conv(jnp.concatenate([qb, kb, vb], axis=-1), conv_w))
    qb, kb, vb = split_cols(conv, (GDN_HEADS * GDN_DK, GDN_HEADS * GDN_DK, GDN_HEADS * GDN_DV))
    beta = jax.nn.sigmoid(bb.astype(jnp.float32))
    log_g = -jnp.exp(a_log.astype(jnp.float32)) * jax.nn.softplus(ab.astype(jnp.float32) + dt_bias.astype(jnp.float32))
    ob = gated_delta_rule(l2norm(qb.reshape(B, T, GDN_HEADS, GDN_DK)), l2norm(kb.reshape(B, T, GDN_HEADS, GDN_DK)),
                          vb.reshape(B, T, GDN_HEADS, GDN_DV), beta, log_g)
    ob = rmsnorm(ob, gdn_norm_w) * jax.nn.silu(zb.reshape(B, T, GDN_HEADS, GDN_DV).astype(jnp.float32))
    mixed = jnp.concatenate([oa.reshape(B, T, -1), ob.reshape(B, T, -1)], axis=-1).astype(x.dtype)
    return mixed @ w_out


def odd_mixer(x, w_in, sinks, lq1, lk1, lq2, lk2, subln_w, w_out, lambda_init):
    B, T, _ = x.shape
    qc, kc, vc, qd, kd, vd = split_cols(x @ w_in, ODD_COLS)
    oc = sliding_window_sink_attention(qc.reshape(B, T, SWA_Q_HEADS, SWA_HEAD_DIM),
                                       kc.reshape(B, T, SWA_KV_HEADS, SWA_HEAD_DIM),
                                       vc.reshape(B, T, SWA_KV_HEADS, SWA_HEAD_DIM), sinks)
    lam = (jnp.exp(jnp.sum(lq1.astype(jnp.float32) * lk1.astype(jnp.float32)))
           - jnp.exp(jnp.sum(lq2.astype(jnp.float32) * lk2.astype(jnp.float32))) + lambda_init)
    od = differential_attention(qd.reshape(B, T, DIFF_HEADS, 2, DIFF_HEAD_DIM), kd.reshape(B, T, DIFF_HEADS, 2, DIFF_HEAD_DIM),
                                vd.reshape(B, T, DIFF_HEADS, 2 * DIFF_HEAD_DIM), lam, subln_w, lambda_init)
    return jnp.concatenate([oc, od], axis=-1) @ w_out


def swiglu(x, w_gate, w_up, w_down):
    return (jax.nn.silu(x @ w_gate) * (x @ w_up)) @ w_down


def moe_swiglu(x, w_router, w_gate, w_up, w_down):
    B, T, D = x.shape
    xt = x.reshape(B * T, D)
    logits = (xt @ w_router).astype(jnp.float32)
    top_val, top_idx = lax.top_k(logits, TOP_K)
    top_w = jax.nn.softmax(top_val, axis=-1)
    gates = jnp.sum(jax.nn.one_hot(top_idx, N_EXPERTS, dtype=jnp.float32) * top_w[..., None], axis=1)
    y = jnp.zeros((B * T, D), jnp.float32)
    for e in range(N_EXPERTS):
        y = y + gates[:, e:e + 1] * swiglu(xt, w_gate[e], w_up[e], w_down[e])
    return y.reshape(B, T, D).astype(x.dtype)


def diff_lambda_init(layer):
    return 0.8 - 0.6 * math.exp(-0.3 * layer)


def setup_inputs(seed: int = 0) -> dict:
    key = jax.random.key(seed)
    k = jax.random.split(key, 32)
    f32 = jnp.float32
    E, O = N_EVEN, N_ODD

    def nrm(kk, shape, scale):
        return jax.random.normal(kk, shape, f32) * scale

    def gain(kk, shape):
        return 1.0 + nrm(kk, shape, 0.02)

    dt = jnp.exp(jax.random.uniform(k[8], (E, GDN_HEADS), f32, minval=math.log(1e-3), maxval=math.log(1e-1)))
    dt_bias = dt + jnp.log(-jnp.expm1(-dt))
    a_log = jnp.log(jax.random.uniform(k[7], (E, GDN_HEADS), f32, minval=1.0, maxval=16.0))
    return {
        'x': nrm(k[0], (BATCH, SEQ, D_MODEL), 1.0),
        'ev_norm_mix': gain(k[1], (E, D_MODEL)),
        'ev_w_in': nrm(k[2], (E, D_MODEL, EVEN_IN), D_MODEL ** -0.5),
        'gla_w_gate_up': nrm(k[3], (E, GLA_GATE_RANK, GLA_HEADS * GLA_DK), GLA_GATE_RANK ** -0.5),
        'gla_b_gate': nrm(k[4], (E, GLA_HEADS * GLA_DK), 0.1),
        'gla_norm_w': gain(k[5], (E, GLA_DV)),
        'gdn_conv_w': nrm(k[6], (E, CONV_WIDTH, GDN_CONV_CH), CONV_WIDTH ** -0.5),
        'gdn_a_log': a_log,
        'gdn_dt_bias': dt_bias,
        'gdn_norm_w': gain(k[9], (E, GDN_DV)),
        'ev_w_out': nrm(k[10], (E, EVEN_MIX, D_MODEL), EVEN_MIX ** -0.5),
        'ev_norm_ffn': gain(k[11], (E, D_MODEL)),
        'ffn_w_gate': nrm(k[12], (E, D_MODEL, D_FF), D_MODEL ** -0.5),
        'ffn_w_up': nrm(k[13], (E, D_MODEL, D_FF), D_MODEL ** -0.5),
        'ffn_w_down': nrm(k[14], (E, D_FF, D_MODEL), D_FF ** -0.5),
        'od_norm_mix': gain(k[15], (O, D_MODEL)),
        'od_w_in': nrm(k[16], (O, D_MODEL, ODD_IN), D_MODEL ** -0.5),
        'swa_sinks': nrm(k[17], (O, SWA_Q_HEADS), 0.5),
        'diff_lambda_q1': nrm(k[18], (O, DIFF_HEAD_DIM), 0.1),
        'diff_lambda_k1': nrm(k[19], (O, DIFF_HEAD_DIM), 0.1),
        'diff_lambda_q2': nrm(k[20], (O, DIFF_HEAD_DIM), 0.1),
        'diff_lambda_k2': nrm(k[21], (O, DIFF_HEAD_DIM), 0.1),
        'diff_subln_w': gain(k[22], (O, 2 * DIFF_HEAD_DIM)),
        'od_w_out': nrm(k[23], (O, ODD_MIX, D_MODEL), ODD_MIX ** -0.5),
        'od_norm_ffn': gain(k[24], (O, D_MODEL)),
        'moe_w_router': nrm(k[25], (O, D_MODEL, N_EXPERTS), D_MODEL ** -0.5),
        'moe_w_gate': nrm(k[26], (O, N_EXPERTS, D_MODEL, D_EXPERT), D_MODEL ** -0.5),
        'moe_w_up': nrm(k[27], (O, N_EXPERTS, D_MODEL, D_EXPERT), D_MODEL ** -0.5),
        'moe_w_down': nrm(k[28], (O, N_EXPERTS, D_EXPERT, D_MODEL), D_EXPERT ** -0.5),
        'final_norm': gain(k[29], (D_MODEL,)),
    }


def reference(x, ev_norm_mix, ev_w_in, gla_w_gate_up, gla_b_gate, gla_norm_w, gdn_conv_w, gdn_a_log, gdn_dt_bias,
              gdn_norm_w, ev_w_out, ev_norm_ffn, ffn_w_gate, ffn_w_up, ffn_w_down,
              od_norm_mix, od_w_in, swa_sinks, diff_lambda_q1, diff_lambda_k1, diff_lambda_q2, diff_lambda_k2,
              diff_subln_w, od_w_out, od_norm_ffn, moe_w_router, moe_w_gate, moe_w_up, moe_w_down, final_norm):
    h = x
    for layer in range(DEPTH):
        i = layer // 2
        if layer % 2 == 0:
            h = h + even_mixer(rmsnorm(h, ev_norm_mix[i]), ev_w_in[i], gla_w_gate_up[i], gla_b_gate[i], gla_norm_w[i],
                               gdn_conv_w[i], gdn_a_log[i], gdn_dt_bias[i], gdn_norm_w[i], ev_w_out[i])
            h = h + swiglu(rmsnorm(h, ev_norm_ffn[i]), ffn_w_gate[i], ffn_w_up[i], ffn_w_down[i])
        else:
            h = h + odd_mixer(rmsnorm(h, od_norm_mix[i]), od_w_in[i], swa_sinks[i], diff_lambda_q1[i], diff_lambda_k1[i],
                              diff_lambda_q2[i], diff_lambda_k2[i], diff_subln_w[i], od_w_out[i], diff_lambda_init(layer))
            h = h + moe_swiglu(rmsnorm(h, od_norm_ffn[i]), moe_w_router[i], moe_w_gate[i], moe_w_up[i], moe_w_down[i])
    return rmsnorm(h, final_norm)
```

```python
import functools
import math

import jax
import jax.numpy as jnp
from jax import lax
from jax.experimental import pallas as pl
from jax.experimental.pallas import tpu as pltpu

F32 = jnp.float32
BF16 = jnp.bfloat16

D_MODEL = 2048
GLA_HEADS = 8
GLA_DK = 64
GLA_DV = 128
GLA_GATE_RANK = 16
GLA_GATE_TEMP = 16.0
GDN_HEADS = 8
GDN_DK = 128
GDN_DV = 128
CONV_WIDTH = 4
CHUNK = 64
SUB = 16
SWA_Q_HEADS = 16
SWA_KV_HEADS = 2
SWA_HEAD_DIM = 64
WINDOW = 128
DIFF_HEADS = 8
DIFF_HEAD_DIM = 64
D_FF = 5632
N_EXPERTS = 8
TOP_K = 2
D_EXPERT = 7168
EPS = 1e-6
NEG = -1e30

LANE = 128
VMEM_LIMIT = 56 * 1024 * 1024

GLA_KD = GLA_HEADS * GLA_DK
GLA_VD = GLA_HEADS * GLA_DV
GDN_D = GDN_HEADS * GDN_DK
EVEN_MAIN = 2 * GLA_KD + 2 * GLA_VD + 4 * GDN_D
OFF_QA, OFF_KA, OFF_VA, OFF_GA = 0, GLA_KD, 2 * GLA_KD, 2 * GLA_KD + GLA_VD
OFF_QB = 2 * GLA_KD + 2 * GLA_VD
OFF_KB, OFF_VB, OFF_ZB = OFF_QB + GDN_D, OFF_QB + 2 * GDN_D, OFF_QB + 3 * GDN_D
MISC_RA, MISC_BB, MISC_AB = 0, GLA_GATE_RANK, GLA_GATE_RANK + GDN_HEADS

SWA_QD = SWA_Q_HEADS * SWA_HEAD_DIM
SWA_KVD = SWA_KV_HEADS * SWA_HEAD_DIM
DIFF_D = DIFF_HEADS * 2 * DIFF_HEAD_DIM
OFF_QC, OFF_KC, OFF_VC = 0, SWA_QD, SWA_QD + SWA_KVD
OFF_QD = SWA_QD + 2 * SWA_KVD
OFF_KD, OFF_VD = OFF_QD + DIFF_D, OFF_QD + 2 * DIFF_D
ODD_IN = OFF_VD + DIFF_D


def _cparams(sem):
    return pltpu.CompilerParams(dimension_semantics=sem, vmem_limit_bytes=VMEM_LIMIT)


def _dot(a, b):
    return jnp.dot(a, b, preferred_element_type=F32)


def _dot_nt(a, b):
    return lax.dot_general(a, b, (((1,), (1,)), ((), ())), preferred_element_type=F32)


def _dot_tn(a, b):
    return lax.dot_general(a, b, (((0,), (0,)), ((), ())), preferred_element_type=F32)


def _split2(x):
    hi = x.astype(BF16)
    lo = (x - hi.astype(F32)).astype(BF16)
    return hi, lo


def _dot_x3(a, b):
    ah, al = _split2(a)
    bh, bl = _split2(b)
    return _dot(ah, bh) + (_dot(ah, bl) + _dot(al, bh))


def _dot_exact_lhs(a_bf16, b):
    bh, bl = _split2(b)
    return _dot(a_bf16, bh) + _dot(a_bf16, bl)


def _sigmoid(x):
    return 1.0 / (1.0 + jnp.exp(-x))


def _silu(x):
    return x * _sigmoid(x)


def _rms(x, w):
    return x * lax.rsqrt(jnp.mean(x * x, axis=-1, keepdims=True) + EPS) * w


def _iota(shape, dim):
    return lax.broadcasted_iota(jnp.int32, shape, dim)


def _mm_body(*refs, n_parts, has_norm, has_res):
    x_refs = refs[:n_parts]
    pos = n_parts
    nw_ref = None
    if has_norm:
        nw_ref = refs[pos]
        pos += 1
    w_refs = refs[pos:pos + n_parts]
    pos += n_parts
    res_ref = None
    if has_res:
        res_ref = refs[pos]
        pos += 1
    o_ref = refs[pos]
    xs_ref = refs[pos + 1]

    @pl.when(pl.program_id(1) == 0)
    def _():
        for p in range(n_parts):
            x = x_refs[p][...].astype(F32)
            if has_norm:
                x = _rms(x, nw_ref[...])
            xs_ref[p] = x.astype(BF16)

    acc = _dot(xs_ref[0], w_refs[0][...].astype(BF16))
    for p in range(1, n_parts):
        acc = acc + _dot(xs_ref[p], w_refs[p][...].astype(BF16))
    if has_res:
        acc = acc + res_ref[...]
    o_ref[...] = acc.astype(o_ref.dtype)


def _matmul(xs, w, *, norm_w=None, res=None, out_dtype=F32, tm=512, tn=512):
    n_parts = len(xs)
    m, kp = xs[0].shape
    n = w.shape[1]
    assert w.shape[0] == kp * n_parts and m % tm == 0 and n % tn == 0
    in_specs = [pl.BlockSpec((tm, kp), lambda i, j: (i, 0)) for _ in range(n_parts)]
    args = list(xs)
    if norm_w is not None:
        assert n_parts == 1
        in_specs.append(pl.BlockSpec((1, kp), lambda i, j: (0, 0)))
        args.append(norm_w.reshape(1, kp))
    for p in range(n_parts):
        in_specs.append(pl.BlockSpec((kp, tn), lambda i, j, p=p: (p, j)))
        args.append(w)
    if res is not None:
        in_specs.append(pl.BlockSpec((tm, tn), lambda i, j: (i, j)))
        args.append(res)
    body = functools.partial(_mm_body, n_parts=n_parts, has_norm=norm_w is not None,
                             has_res=res is not None)
    return pl.pallas_call(
        body,
        grid=(m // tm, n // tn),
        in_specs=in_specs,
        out_specs=pl.BlockSpec((tm, tn), lambda i, j: (i, j)),
        out_shape=jax.ShapeDtypeStruct((m, n), out_dtype),
        scratch_shapes=[pltpu.VMEM((n_parts, tm, kp), BF16)],
        compiler_params=_cparams(("parallel", "arbitrary")),
        name="matmul",
    )(*args)


def _swiglu_step(xs, wg_ref, wu_ref, wd_ref, acc_ref):
    g = _dot(xs, wg_ref[...].astype(BF16))
    u = _dot(xs, wu_ref[...].astype(BF16))
    hmid = (_silu(g) * u).astype(BF16)
    acc_ref[...] += _dot(hmid, wd_ref[...].astype(BF16))


def _ffn_body(x_ref, nw_ref, wg_ref, wu_ref, wd_ref, o_ref, xs_ref, acc_ref):
    j = pl.program_id(1)

    @pl.when(j == 0)
    def _():
        xs_ref[...] = _rms(x_ref[...], nw_ref[...]).astype(BF16)
        acc_ref[...] = jnp.zeros_like(acc_ref)

    _swiglu_step(xs_ref[...], wg_ref, wu_ref, wd_ref, acc_ref)

    @pl.when(j == pl.num_programs(1) - 1)
    def _():
        o_ref[...] = x_ref[...] + acc_ref[...]


def _ffn(h, norm_w, w_gate, w_up, w_down, *, tm=512, tf=512):
    m, d = h.shape
    f = w_gate.shape[1]
    assert m % tm == 0 and f % tf == 0
    return pl.pallas_call(
        _ffn_body,
        grid=(m // tm, f // tf),
        in_specs=[
            pl.BlockSpec((tm, d), lambda i, j: (i, 0)),
            pl.BlockSpec((1, d), lambda i, j: (0, 0)),
            pl.BlockSpec((d, tf), lambda i, j: (0, j)),
            pl.BlockSpec((d, tf), lambda i, j: (0, j)),
            pl.BlockSpec((tf, d), lambda i, j: (j, 0)),
        ],
        out_specs=pl.BlockSpec((tm, d), lambda i, j: (i, 0)),
        out_shape=jax.ShapeDtypeStruct((m, d), F32),
        scratch_shapes=[pltpu.VMEM((tm, d), BF16), pltpu.VMEM((tm, d), F32)],
        compiler_params=_cparams(("parallel", "arbitrary")),
        name="ffn",
    )(h, norm_w.reshape(1, d), w_gate, w_up, w_down)


def _moe_body(te_ref, nu_ref, x_ref, wg_ref, wu_ref, wd_ref, o_ref, xs_ref, acc_ref):
    i = pl.program_id(0)
    j = pl.program_id(1)
    used = i < nu_ref[0]

    @pl.when(used & (j == 0))
    def _():
        xs_ref[...] = x_ref[...].astype(BF16)
        acc_ref[...] = jnp.zeros_like(acc_ref)

    @pl.when(used)
    def _():
        _swiglu_step(xs_ref[...], wg_ref, wu_ref, wd_ref, acc_ref)

    @pl.when(used & (j == pl.num_programs(1) - 1))
    def _():
        o_ref[...] = acc_ref[...]

    @pl.when(jnp.logical_not(used) & (j == 0))
    def _():
        o_ref[...] = jnp.zeros_like(o_ref)


def _moe_experts(xs, tile_expert, n_used, w_gate, w_up, w_down, *, tm, tf):
    p, d = xs.shape
    f = w_gate.shape[2]
    n_tiles = p // tm
    nj = f // tf

    def row_map(i, j, te, nu):
        return (jnp.minimum(i, nu[0] - 1), 0)

    def _eff(i, j, te, nu):
        used = i < nu[0]
        ie = jnp.minimum(i, nu[0] - 1)
        je = jnp.where(used, j, nj - 1)
        return te[ie], je

    def gu_map(i, j, te, nu):
        e, je = _eff(i, j, te, nu)
        return (e, 0, je)

    def dn_map(i, j, te, nu):
        e, je = _eff(i, j, te, nu)
        return (e, je, 0)

    grid_spec = pltpu.PrefetchScalarGridSpec(
        num_scalar_prefetch=2,
        grid=(n_tiles, nj),
        in_specs=[
            pl.BlockSpec((tm, d), row_map),
            pl.BlockSpec((None, d, tf), gu_map),
            pl.BlockSpec((None, d, tf), gu_map),
            pl.BlockSpec((None, tf, d), dn_map),
        ],
        out_specs=pl.BlockSpec((tm, d), lambda i, j, te, nu: (i, 0)),
        scratch_shapes=[pltpu.VMEM((tm, d), BF16), pltpu.VMEM((tm, d), F32)],
    )
    return pl.pallas_call(
        _moe_body,
        grid_spec=grid_spec,
        out_shape=jax.ShapeDtypeStruct((p, d), F32),
        compiler_params=_cparams(("arbitrary", "arbitrary")),
        name="moe_experts",
    )(tile_expert, n_used, xs, w_gate, w_up, w_down)


def _gla_body(q_ref, k_ref, v_ref, g_ref, misc_ref, wg_ref, bg_ref, nw_ref, o_ref, st_ref, *, tb):
    @pl.when(pl.program_id(2) == 0)
    def _():
        st_ref[...] = jnp.zeros_like(st_ref)

    c = CHUNK
    two_dk = 2 * GLA_DK
    ra = misc_ref[:, MISC_RA:MISC_RA + GLA_GATE_RANK]
    z = _dot_x3(ra, wg_ref[...]) + bg_ref[...]
    log_a = (jnp.minimum(z, 0.0) - jnp.log(1.0 + jnp.exp(-jnp.abs(z)))) * (1.0 / GLA_GATE_TEMP)

    row = _iota((c, c), 0)
    col = _iota((c, c), 1)
    tril = (row >= col).astype(BF16)
    rblk, cblk = row >> 4, col >> 4
    lane = _iota((1, two_dk), 1)
    head_mask = [(lane < GLA_DK).astype(F32), (lane >= GLA_DK).astype(F32)]
    st_mask = ((_iota((2 * GLA_DV, two_dk), 0) >= GLA_DV) == (_iota((2 * GLA_DV, two_dk), 1) >= GLA_DK))
    nsub = c // SUB

    for ci in range(tb // c):
        rows = slice(ci * c, (ci + 1) * c)
        q = q_ref[rows, :] * (GLA_DK ** -0.5)
        k = k_ref[rows, :]
        v = v_ref[rows, :]
        b = _dot_exact_lhs(tril, log_a[rows, :])
        b4 = b.reshape(nsub, SUB, two_dk)
        bend4 = jnp.broadcast_to(b4[:, SUB - 1:SUB, :], b4.shape)
        k_hat = k * jnp.exp(bend4.reshape(c, two_dk) - b)
        b_end = b[c - 1:c, :]
        st = st_ref[...]

        lhs_parts = []
        for jb in range(nsub - 1):
            bj = b[jb * SUB + SUB - 1:jb * SUB + SUB, :]
            lhs_parts.append(q * jnp.exp(jnp.minimum(b - bj, 0.0)))
        lhs_all = jnp.concatenate(lhs_parts, axis=0)

        q4 = q.reshape(nsub, SUB, two_dk)
        k4 = k.reshape(nsub, SUB, two_dk)
        diag = [jnp.zeros((c, c), F32), jnp.zeros((c, c), F32)]
        cloc = col - (rblk << 4)
        for jj in range(SUB):
            kj = k4[:, jj:jj + 1, :]
            bj = b4[:, jj:jj + 1, :]
            prod = (q4 * kj * jnp.exp(jnp.minimum(b4 - bj, 0.0))).reshape(c, two_dk)
            for hh in range(2):
                colv = jnp.sum(prod * head_mask[hh], axis=-1, keepdims=True)
                diag[hh] = jnp.where(cloc == jj, colv, diag[hh])

        o_inter = _dot_nt((q * jnp.exp(b)).astype(BF16), st.astype(BF16))
        outs = []
        for hh in range(2):
            off = _dot_nt((lhs_all * head_mask[hh]).astype(BF16), k_hat.astype(BF16))
            s_off = jnp.where(cblk == 0, off[0:c], jnp.where(cblk == 1, off[c:2 * c], off[2 * c:3 * c]))
            scores = jnp.where(rblk > cblk, s_off, jnp.where((rblk == cblk) & (row >= col), diag[hh], 0.0))
            vh = v[:, hh * GLA_DV:(hh + 1) * GLA_DV]
            o = _dot(scores.astype(BF16), vh.astype(BF16)) + o_inter[:, hh * GLA_DV:(hh + 1) * GLA_DV]
            gate = _silu(g_ref[rows, hh * GLA_DV:(hh + 1) * GLA_DV])
            outs.append(_rms(o, nw_ref[...]) * gate)
        o_ref[rows, :] = jnp.concatenate(outs, axis=1).astype(o_ref.dtype)

        k_dec = k * jnp.exp(b_end - b)
        upd = _dot_tn(v.astype(BF16), k_dec.astype(BF16))
        st_ref[...] = st * jnp.exp(b_end) + jnp.where(st_mask, upd, 0.0)


def _gla(p_main, p_misc, w_gate_up, b_gate, norm_w, *, batch, seq, tb=256):
    m = batch * seq
    nt = seq // tb
    hp = GLA_HEADS // 2
    two_dk, two_dv = 2 * GLA_DK, 2 * GLA_DV

    def rows(b, h, t):
        return b * nt + t

    body = functools.partial(_gla_body, tb=tb)
    return pl.pallas_call(
        body,
        grid=(batch, hp, nt),
        in_specs=[
            pl.BlockSpec((tb, two_dk), lambda b, h, t: (rows(b, h, t), OFF_QA // two_dk + h)),
            pl.BlockSpec((tb, two_dk), lambda b, h, t: (rows(b, h, t), OFF_KA // two_dk + h)),
            pl.BlockSpec((tb, two_dv), lambda b, h, t: (rows(b, h, t), OFF_VA // two_dv + h)),
            pl.BlockSpec((tb, two_dv), lambda b, h, t: (rows(b, h, t), OFF_GA // two_dv + h)),
            pl.BlockSpec((tb, LANE), lambda b, h, t: (rows(b, h, t), 0)),
            pl.BlockSpec((GLA_GATE_RANK, two_dk), lambda b, h, t: (0, h)),
            pl.BlockSpec((1, two_dk), lambda b, h, t: (0, h)),
            pl.BlockSpec((1, GLA_DV), lambda b, h, t: (0, 0)),
        ],
        out_specs=pl.BlockSpec((tb, two_dv), lambda b, h, t: (rows(b, h, t), h)),
        out_shape=jax.ShapeDtypeStruct((m, GLA_VD), BF16),
        scratch_shapes=[pltpu.VMEM((two_dv, two_dk), F32)],
        compiler_params=_cparams(("parallel", "parallel", "arbitrary")),
        name="gla",
    )(p_main, p_main, p_main, p_main, p_misc, w_gate_up, b_gate.reshape(1, -1), norm_w.reshape(1, -1))


GDN_GROUP = 4
GDN_ROWS = GDN_GROUP * CHUNK
GDN_PAD = 8


def _gdn_body(q_ref, k_ref, v_ref, z_ref, misc_ref, cwq_ref, cwk_ref, cwv_ref, alog_ref, dtb_ref, nw_ref,
              o_ref, xbuf_ref, st_ref):
    c = CHUNK
    gw = GDN_GROUP * GDN_DK
    grp = pl.program_id(1)

    @pl.when(pl.program_id(2) == 0)
    def _():
        st_ref[...] = jnp.zeros_like(st_ref)
        xbuf_ref[0:GDN_PAD, :] = jnp.zeros((GDN_PAD, 3 * gw), F32)

    xbuf_ref[GDN_PAD:GDN_PAD + c, 0:gw] = q_ref[...]
    xbuf_ref[GDN_PAD:GDN_PAD + c, gw:2 * gw] = k_ref[...]
    xbuf_ref[GDN_PAD:GDN_PAD + c, 2 * gw:3 * gw] = v_ref[...]
    cw = jnp.concatenate([cwq_ref[...], cwk_ref[...], cwv_ref[...]], axis=1)
    conv = jnp.zeros((c, 3 * gw), F32)
    for i in range(CONV_WIDTH):
        start = GDN_PAD - (CONV_WIDTH - 1) + i
        conv = conv + xbuf_ref[start:start + c, :] * cw[i:i + 1, :]
    xbuf_ref[0:GDN_PAD, :] = xbuf_ref[c:c + GDN_PAD, :]
    conv = _silu(conv)

    def l2n(x):
        return x * lax.rsqrt(jnp.sum(x * x, axis=-1, keepdims=True) + EPS)

    q_parts, k_parts, v_parts = [], [], []
    for h in range(GDN_GROUP):
        q_parts.append(l2n(conv[:, h * GDN_DK:(h + 1) * GDN_DK]) * (GDN_DK ** -0.5))
        k_parts.append(l2n(conv[:, gw + h * GDN_DK:gw + (h + 1) * GDN_DK]))
        v_parts.append(conv[:, 2 * gw + h * GDN_DV:2 * gw + (h + 1) * GDN_DV])
    qst = jnp.concatenate(q_parts, axis=0)
    kst = jnp.concatenate(k_parts, axis=0)
    vst = jnp.concatenate(v_parts, axis=0)

    misc = misc_ref[...]
    beta_all = _sigmoid(misc)
    sp_in = misc + dtb_ref[...]
    softplus = jnp.maximum(sp_in, 0.0) + jnp.log(1.0 + jnp.exp(-jnp.abs(sp_in)))
    log_g_all = -jnp.exp(alog_ref[...]) * softplus
    row64 = _iota((c, c), 0)
    col64 = _iota((c, c), 1)
    tril64 = (row64 >= col64).astype(BF16)
    gam_all = _dot_exact_lhs(tril64, log_g_all)

    lane = _iota((c, LANE), 1)

    def col_of(x, idx):
        return jnp.sum(jnp.where(lane == idx, x, 0.0), axis=-1, keepdims=True)

    beta_parts, gam_parts, gend_parts = [], [], []
    for h in range(GDN_GROUP):
        head = grp * GDN_GROUP + h
        beta_parts.append(col_of(beta_all, MISC_BB + head))
        gcol = col_of(gam_all, MISC_AB + head)
        gam_parts.append(gcol)
        gend_parts.append(jnp.broadcast_to(gcol[c - 1:c, :], (c, 1)))
    beta = jnp.concatenate(beta_parts, axis=0)
    gam = jnp.concatenate(gam_parts, axis=0)
    gend = jnp.concatenate(gend_parts, axis=0)

    n = GDN_ROWS
    row = _iota((n, n), 0)
    col = _iota((n, n), 1)
    same_head = (row >> 6) == (col >> 6)
    gam_b = jnp.broadcast_to(gam, (n, n))
    gdiff = jnp.minimum(gam_b - gam_b.T, 0.0)
    decay = jnp.exp(gdiff)
    kb = kst.astype(BF16)
    gram = _dot_nt(kb, kb)
    a_mat = jnp.where(same_head & (row > col), beta * gram * decay, 0.0)

    inv = jnp.where(row == col, 1.0, 0.0) - jnp.where(((row ^ col) == 1) & (row > col), a_mat, 0.0)
    for lg in range(1, 6):
        s = 1 << lg
        sel = ((row >> (lg + 1)) == (col >> (lg + 1))) & ((row & s) != 0) & ((col & s) == 0)
        a_s = jnp.where(sel, a_mat, 0.0)
        inv = inv - _dot_x3(_dot_x3(inv, a_s), inv)

    rk = (beta * jnp.exp(gam)) * kst
    rv = beta * vst
    wu = _dot_x3(inv, jnp.concatenate([rk, rv], axis=1))
    w_mat, u0 = wu[:, :GDN_DK], wu[:, GDN_DK:]
    qk = jnp.where(same_head & (row >= col), _dot_nt(qst.astype(BF16), kb) * decay, 0.0)
    qg = (qst * jnp.exp(gam)).astype(BF16)
    kd = (kst * jnp.exp(gend - gam)).astype(BF16)
    ge = jnp.exp(gend)

    u_parts, oi_parts = [], []
    for h in range(GDN_GROUP):
        r = slice(h * c, (h + 1) * c)
        st = st_ref[h]
        stb = st.astype(BF16)
        u = u0[r] - _dot_nt(w_mat[r].astype(BF16), stb)
        oi_parts.append(_dot_nt(qg[r], stb))
        st_ref[h] = st * ge[h * c:h * c + 1, :] + _dot_tn(u.astype(BF16), kd[r])
        u_parts.append(u)
    ust = jnp.concatenate(u_parts, axis=0)
    o = jnp.concatenate(oi_parts, axis=0) + _dot(qk.astype(BF16), ust.astype(BF16))

    for h in range(GDN_GROUP):
        oh = _rms(o[h * c:(h + 1) * c], nw_ref[...]) * _silu(z_ref[:, h * GDN_DV:(h + 1) * GDN_DV])
        o_ref[:, h * GDN_DV:(h + 1) * GDN_DV] = oh.astype(o_ref.dtype)


def _gdn(p_main, p_misc, conv_w, a_log, dt_bias, norm_w, *, batch, seq):
    m = batch * seq
    nc = seq // CHUNK
    ng = GDN_HEADS // GDN_GROUP
    gw = GDN_GROUP * GDN_DK

    def rows(b, g, t):
        return b * nc + t

    alog_l = jnp.zeros((1, LANE), F32).at[0, MISC_AB:MISC_AB + GDN_HEADS].set(a_log)
    dtb_l = jnp.zeros((1, LANE), F32).at[0, MISC_AB:MISC_AB + GDN_HEADS].set(dt_bias)
    return pl.pallas_call(
        _gdn_body,
        grid=(batch, ng, nc),
        in_specs=[
            pl.BlockSpec((CHUNK, gw), lambda b, g, t: (rows(b, g, t), OFF_QB // gw + g)),
            pl.BlockSpec((CHUNK, gw), lambda b, g, t: (rows(b, g, t), OFF_KB // gw + g)),
            pl.BlockSpec((CHUNK, gw), lambda b, g, t: (rows(b, g, t), OFF_VB // gw + g)),
            pl.BlockSpec((CHUNK, gw), lambda b, g, t: (rows(b, g, t), OFF_ZB // gw + g)),
            pl.BlockSpec((CHUNK, LANE), lambda b, g, t: (rows(b, g, t), 0)),
            pl.BlockSpec((CONV_WIDTH, gw), lambda b, g, t: (0, g)),
            pl.BlockSpec((CONV_WIDTH, gw), lambda b, g, t: (0, ng + g)),
            pl.BlockSpec((CONV_WIDTH, gw), lambda b, g, t: (0, 2 * ng + g)),
            pl.BlockSpec((1, LANE), lambda b, g, t: (0, 0)),
            pl.BlockSpec((1, LANE), lambda b, g, t: (0, 0)),
            pl.BlockSpec((1, GDN_DV), lambda b, g, t: (0, 0)),
        ],
        out_specs=pl.BlockSpec((CHUNK, gw), lambda b, g, t: (rows(b, g, t), g)),
        out_shape=jax.ShapeDtypeStruct((m, GDN_D), BF16),
        scratch_shapes=[pltpu.VMEM((CHUNK + GDN_PAD, 3 * gw), F32),
                        pltpu.VMEM((GDN_GROUP, GDN_DV, GDN_DK), F32)],
        compiler_params=_cparams(("parallel", "parallel", "arbitrary")),
        name="gdn",
    )(p_main, p_main, p_main, p_main, p_misc, conv_w, conv_w, conv_w, alog_l, dtb_l, norm_w.reshape(1, -1))


def _swa_body(sink_ref, q_ref, kp_ref, kc_ref, vp_ref, vc_ref, o_ref):
    w = WINDOW
    n = pl.program_id(1)
    g = SWA_Q_HEADS // SWA_KV_HEADS
    kk = jnp.concatenate([kp_ref[...], kc_ref[...]], axis=0)
    vv = jnp.concatenate([vp_ref[...], vc_ref[...]], axis=0)
    lane = _iota((1, LANE), 1)
    qpos = _iota((w, 2 * w), 0) + w
    kpos = _iota((w, 2 * w), 1)
    mask = (kpos <= qpos) & (kpos > qpos - w) & ((kpos >= w) | (n > 0))
    scale = SWA_HEAD_DIM ** -0.5
    for hk in range(SWA_KV_HEADS):
        in_head = (lane >= hk * SWA_HEAD_DIM) & (lane < (hk + 1) * SWA_HEAD_DIM)
        k_same = jnp.where(in_head, kk, 0.0)
        v_same = jnp.where(in_head, vv, 0.0)
        k_other = pltpu.roll(k_same, SWA_HEAD_DIM, 1)
        v_other = pltpu.roll(v_same, SWA_HEAD_DIM, 1)
        k_half = [k_same, k_other] if hk == 0 else [k_other, k_same]
        v_half = [v_same, v_other] if hk == 0 else [v_other, v_same]
        for pair in range(g // 2):
            blk = hk * (g // 2) + pair
            q2 = (q_ref[:, blk * LANE:(blk + 1) * LANE] * scale).astype(BF16)
            acc = jnp.zeros((w, LANE), F32)
            for half in range(2):
                head = 2 * blk + half
                s = _dot_nt(q2, k_half[half].astype(BF16))
                s = jnp.where(mask, s, NEG)
                sink = sink_ref[head]
                mx = jnp.maximum(jnp.max(s, axis=-1, keepdims=True), sink)
                p = jnp.exp(s - mx)
                denom = jnp.sum(p, axis=-1, keepdims=True) + jnp.exp(sink - mx)
                p = p / denom
                acc = acc + _dot(p.astype(BF16), v_half[half].astype(BF16))
            o_ref[:, blk * LANE:(blk + 1) * LANE] = acc.astype(o_ref.dtype)


def _swa(p2, sinks, *, batch, seq):
    m = batch * seq
    nb = seq // WINDOW

    def cur(b, n):
        return b * nb + n

    def prev(b, n):
        return b * nb + jnp.maximum(n - 1, 0)

    kcol = OFF_KC // LANE
    vcol = OFF_VC // LANE
    return pl.pallas_call(
        _swa_body,
        grid=(batch, nb),
        in_specs=[
            pl.BlockSpec(memory_space=pltpu.SMEM),
            pl.BlockSpec((WINDOW, SWA_QD), lambda b, n: (cur(b, n), 0)),
            pl.BlockSpec((WINDOW, LANE), lambda b, n: (prev(b, n), kcol)),
            pl.BlockSpec((WINDOW, LANE), lambda b, n: (cur(b, n), kcol)),
            pl.BlockSpec((WINDOW, LANE), lambda b, n: (prev(b, n), vcol)),
            pl.BlockSpec((WINDOW, LANE), lambda b, n: (cur(b, n), vcol)),
        ],
        out_specs=pl.BlockSpec((WINDOW, SWA_QD), lambda b, n: (cur(b, n), 0)),
        out_shape=jax.ShapeDtypeStruct((m, SWA_QD), BF16),
        compiler_params=_cparams(("parallel", "arbitrary")),
        name="swa",
    )(sinks, p2, p2, p2, p2, p2)


def _diff_body(q_ref, k_ref, v_ref, lamp_ref, nw_ref, o_ref, m_ref, l_ref, acc_ref, *, tq, lambda_init):
    qi = pl.program_id(2)
    dh = DIFF_HEAD_DIM
    lane = _iota((1, 2 * dh), 1)
    q = q_ref[...] * (dh ** -0.5)
    q_half = [jnp.where(lane < dh, q, 0.0).astype(BF16), jnp.where(lane >= dh, q, 0.0).astype(BF16)]
    m_ref[...] = jnp.full(m_ref.shape, NEG, F32)
    l_ref[...] = jnp.zeros_like(l_ref)
    acc_ref[...] = jnp.zeros_like(acc_ref)
    qpos = qi * tq + _iota((tq, tq), 0)

    def step(kb, carry):
        k = k_ref[pl.ds(pl.multiple_of(kb * tq, tq), tq), :].astype(BF16)
        v = v_ref[pl.ds(pl.multiple_of(kb * tq, tq), tq), :].astype(BF16)
        kpos = kb * tq + _iota((tq, tq), 1)
        visible = kpos <= qpos
        for half in range(2):
            s = jnp.where(visible, _dot_nt(q_half[half], k), NEG)
            m_old = m_ref[half]
            m_new = jnp.maximum(m_old, jnp.max(s, axis=-1, keepdims=True))
            alpha = jnp.exp(m_old - m_new)
            p = jnp.exp(s - m_new)
            l_ref[half] = alpha * l_ref[half] + jnp.sum(p, axis=-1, keepdims=True)
            acc_ref[half] = alpha * acc_ref[half] + _dot(p.astype(BF16), v)
            m_ref[half] = m_new
        return carry

    lax.fori_loop(0, qi + 1, step, 0)

    lp = lamp_ref[...]
    lam = (jnp.exp(jnp.sum(lp[0:1] * lp[1:2], axis=-1, keepdims=True))
           - jnp.exp(jnp.sum(lp[2:3] * lp[3:4], axis=-1, keepdims=True)) + lambda_init)
    o = acc_ref[0] / l_ref[0] - lam * (acc_ref[1] / l_ref[1])
    o_ref[...] = (_rms(o, nw_ref[...]) * (1.0 - lambda_init)).astype(o_ref.dtype)


def _diff_attn(p2, lam_params, subln_w, lambda_init, *, batch, seq, tq=256):
    m = batch * seq
    nq = seq // tq
    hd = 2 * DIFF_HEAD_DIM
    body = functools.partial(_diff_body, tq=tq, lambda_init=lambda_init)
    return pl.pallas_call(
        body,
        grid=(batch, DIFF_HEADS, nq),
        in_specs=[
            pl.BlockSpec((tq, hd), lambda b, h, i: (b * nq + i, OFF_QD // hd + h)),
            pl.BlockSpec((seq, hd), lambda b, h, i: (b, OFF_KD // hd + h)),
            pl.BlockSpec((seq, hd), lambda b, h, i: (b, OFF_VD // hd + h)),
            pl.BlockSpec((4, DIFF_HEAD_DIM), lambda b, h, i: (0, 0)),
            pl.BlockSpec((1, hd), lambda b, h, i: (0, 0)),
        ],
        out_specs=pl.BlockSpec((tq, hd), lambda b, h, i: (b * nq + i, h)),
        out_shape=jax.ShapeDtypeStruct((m, DIFF_D), BF16),
        scratch_shapes=[pltpu.VMEM((2, tq, 1), F32), pltpu.VMEM((2, tq, 1), F32),
                        pltpu.VMEM((2, tq, hd), F32)],
        compiler_params=_cparams(("parallel", "parallel", "arbitrary")),
        name="diff_attn",
    )(p2, p2, p2, lam_params, subln_w.reshape(1, hd))


R_I1, R_I2, R_G1, R_G2 = 0, 1, 2, 3


def _router_body(h_ref, nw_ref, wr_ref, xn_ref, rec_ref):
    xn = _rms(h_ref[...], nw_ref[...])
    xn_ref[...] = xn
    logits = _dot_x3(xn, wr_ref[...])
    lane = _iota(logits.shape, 1)
    lg = jnp.where(lane < N_EXPERTS, logits, NEG)
    m1 = jnp.max(lg, axis=-1, keepdims=True)
    i1 = jnp.min(jnp.where(lg == m1, lane, LANE), axis=-1, keepdims=True)
    lg2 = jnp.where(lane == i1, NEG, lg)
    m2 = jnp.max(lg2, axis=-1, keepdims=True)
    i2 = jnp.min(jnp.where(lg2 == m2, lane, LANE), axis=-1, keepdims=True)
    e = jnp.exp(m2 - m1)
    g1 = 1.0 / (1.0 + e)
    g2 = e / (1.0 + e)
    rec = jnp.where(lane == R_I1, i1.astype(F32),
                    jnp.where(lane == R_I2, i2.astype(F32),
                              jnp.where(lane == R_G1, g1, jnp.where(lane == R_G2, g2, 0.0))))
    rec_ref[...] = rec


def _router(h, norm_w, w_router, *, tm=512):
    m, d = h.shape
    wr = jnp.zeros((d, LANE), F32).at[:, :N_EXPERTS].set(w_router)
    return pl.pallas_call(
        _router_body,
        grid=(m // tm,),
        in_specs=[
            pl.BlockSpec((tm, d), lambda i: (i, 0)),
            pl.BlockSpec((1, d), lambda i: (0, 0)),
            pl.BlockSpec((d, LANE), lambda i: (0, 0)),
        ],
        out_specs=[pl.BlockSpec((tm, d), lambda i: (i, 0)), pl.BlockSpec((tm, LANE), lambda i: (i, 0))],
        out_shape=[jax.ShapeDtypeStruct((m, d), F32), jax.ShapeDtypeStruct((m, LANE), F32)],
        compiler_params=_cparams(("parallel",)),
        name="router",
    )(h, norm_w.reshape(1, d), wr)


def _row_copy(src, s_row, dst, d_row, sem):
    return pltpu.make_async_copy(src.at[pl.ds(s_row, 1), :], dst.at[pl.ds(d_row, 1), :], sem)


def _scatter_body(dest_ref, x_hbm, init_hbm, o_hbm, sem, *, rt):
    del init_hbm
    base = pl.program_id(0) * rt

    def issue(r, carry):
        for k in range(TOP_K):
            _row_copy(x_hbm, base + r, o_hbm, dest_ref[0, 0, TOP_K * r + k], sem).start()
        return carry

    lax.fori_loop(0, rt, issue, 0)

    def drain(r, carry):
        for k in range(TOP_K):
            _row_copy(x_hbm, 0, o_hbm, 0, sem).wait()
        return carry

    lax.fori_loop(0, rt, drain, 0)


def _scatter_rows(xn, dest, p_rows, *, rt=128):
    m, d = xn.shape
    dest3 = dest.reshape(m // rt, 1, TOP_K * rt)
    init = jnp.zeros((p_rows, d), F32)
    body = functools.partial(_scatter_body, rt=rt)
    return pl.pallas_call(
        body,
        grid=(m // rt,),
        in_specs=[
            pl.BlockSpec((1, 1, TOP_K * rt), lambda i: (i, 0, 0), memory_space=pltpu.SMEM),
            pl.BlockSpec(memory_space=pl.ANY),
            pl.BlockSpec(memory_space=pl.ANY),
        ],
        out_specs=pl.BlockSpec(memory_space=pl.ANY),
        out_shape=jax.ShapeDtypeStruct((p_rows, d), F32),
        scratch_shapes=[pltpu.SemaphoreType.DMA(())],
        input_output_aliases={2: 0},
        compiler_params=_cparams(("arbitrary",)),
        name="moe_scatter",
    )(dest3, xn, init)


def _combine_body(dest_ref, y_hbm, h_ref, rec_ref, nw_ref, o_ref, buf_ref, sem, *, rt):
    def issue(r, carry):
        for k in range(TOP_K):
            _row_copy(y_hbm, dest_ref[0, 0, TOP_K * r + k], buf_ref.at[k], r, sem).start()
        return carry

    lax.fori_loop(0, rt, issue, 0)

    def drain(r, carry):
        for k in range(TOP_K):
            _row_copy(y_hbm, 0, buf_ref.at[k], 0, sem).wait()
        return carry

    lax.fori_loop(0, rt, drain, 0)
    rec = rec_ref[...]
    g1 = rec[:, R_G1:R_G1 + 1]
    g2 = rec[:, R_G2:R_G2 + 1]
    hn = h_ref[...] + g1 * buf_ref[0] + g2 * buf_ref[1]
    o_ref[...] = _rms(hn, nw_ref[...])


def _combine(ys, dest, h, rec, final_norm, *, rt=128):
    m, d = h.shape
    dest3 = dest.reshape(m // rt, 1, TOP_K * rt)
    body = functools.partial(_combine_body, rt=rt)
    return pl.pallas_call(
        body,
        grid=(m // rt,),
        in_specs=[
            pl.BlockSpec((1, 1, TOP_K * rt), lambda i: (i, 0, 0), memory_space=pltpu.SMEM),
            pl.BlockSpec(memory_space=pl.ANY),
            pl.BlockSpec((rt, d), lambda i: (i, 0)),
            pl.BlockSpec((rt, LANE), lambda i: (i, 0)),
            pl.BlockSpec((1, d), lambda i: (0, 0)),
        ],
        out_specs=pl.BlockSpec((rt, d), lambda i: (i, 0)),
        out_shape=jax.ShapeDtypeStruct((m, d), F32),
        scratch_shapes=[pltpu.VMEM((TOP_K, rt, d), F32), pltpu.SemaphoreType.DMA(())],
        compiler_params=_cparams(("arbitrary",)),
        name="moe_combine",
    )(dest3, ys, h, rec, final_norm.reshape(1, d))


def _routing_plan(rec, *, tm):
    m = rec.shape[0]
    idx = rec[:, R_I1:R_I2 + 1].astype(jnp.int32)
    flat = idx.reshape(-1)
    onehot = (flat[:, None] == jnp.arange(N_EXPERTS, dtype=jnp.int32)[None, :]).astype(jnp.int32)
    csum = jnp.cumsum(onehot, axis=0)
    rank = jnp.sum(onehot * csum, axis=1) - 1
    counts = csum[-1]
    padded = ((counts + tm - 1) // tm) * tm
    ends = jnp.cumsum(padded)
    starts = ends - padded
    dest = jnp.sum(onehot * starts[None, :], axis=1) + rank
    n_tiles = (TOP_K * m) // tm + N_EXPERTS
    tile_start = jnp.arange(n_tiles, dtype=jnp.int32) * tm
    tile_expert = jnp.minimum(jnp.sum((tile_start[:, None] >= ends[None, :]).astype(jnp.int32), axis=1),
                              N_EXPERTS - 1).astype(jnp.int32)
    n_used = (ends[-1] // tm).astype(jnp.int32).reshape(1)
    return dest.astype(jnp.int32), tile_expert, n_used, n_tiles * tm


def kernel(x, ev_norm_mix, ev_w_in, gla_w_gate_up, gla_b_gate, gla_norm_w, gdn_conv_w, gdn_a_log, gdn_dt_bias,
           gdn_norm_w, ev_w_out, ev_norm_ffn, ffn_w_gate, ffn_w_up, ffn_w_down,
           od_norm_mix, od_w_in, swa_sinks, diff_lambda_q1, diff_lambda_k1, diff_lambda_q2, diff_lambda_k2,
           diff_subln_w, od_w_out, od_norm_ffn, moe_w_router, moe_w_gate, moe_w_up, moe_w_down, final_norm):
    batch, seq, d = x.shape
    m = batch * seq
    h = x.reshape(m, d)

    w_in = ev_w_in[0]
    split = OFF_QB
    gdn_lo = split + GLA_GATE_RANK
    gdn_hi = gdn_lo + 4 * GDN_D
    w_main = jnp.concatenate([w_in[:, :split], w_in[:, gdn_lo:gdn_hi]], axis=1)
    w_misc = jnp.concatenate([w_in[:, split:gdn_lo], w_in[:, gdn_hi:],
                              jnp.zeros((d, LANE - GLA_GATE_RANK - 2 * GDN_HEADS), F32)], axis=1)
    p_main = _matmul((h,), w_main, norm_w=ev_norm_mix[0], out_dtype=F32, tm=512, tn=512)
    p_misc = _matmul((h,), w_misc, norm_w=ev_norm_mix[0], out_dtype=F32, tm=512, tn=LANE)
    oa = _gla(p_main, p_misc, gla_w_gate_up[0], gla_b_gate[0], gla_norm_w[0], batch=batch, seq=seq)
    ob = _gdn(p_main, p_misc, gdn_conv_w[0], gdn_a_log[0], gdn_dt_bias[0], gdn_norm_w[0], batch=batch, seq=seq)
    h = _matmul((oa, ob), ev_w_out[0], res=h, out_dtype=F32, tm=512, tn=512)
    h = _ffn(h, ev_norm_ffn[0], ffn_w_gate[0], ffn_w_up[0], ffn_w_down[0])

    lambda_init = 0.8 - 0.6 * math.exp(-0.3 * 1)
    p2 = _matmul((h,), od_w_in[0], norm_w=od_norm_mix[0], out_dtype=F32, tm=512, tn=256)
    oc = _swa(p2, swa_sinks[0], batch=batch, seq=seq)
    lam_params = jnp.stack([diff_lambda_q1[0], diff_lambda_k1[0], diff_lambda_q2[0], diff_lambda_k2[0]])
    od = _diff_attn(p2, lam_params, diff_subln_w[0], lambda_init, batch=batch, seq=seq)
    h = _matmul((oc, od), od_w_out[0], res=h, out_dtype=F32, tm=512, tn=512)

    moe_tm = 512
    xn, rec = _router(h, od_norm_ffn[0], moe_w_router[0])
    dest, tile_expert, n_used, p_rows = _routing_plan(rec, tm=moe_tm)
    xs = _scatter_rows(xn, dest, p_rows)
    ys = _moe_experts(xs, tile_expert, n_used, moe_w_gate[0], moe_w_up[0], moe_w_down[0], tm=moe_tm, tf=256)
    out = _combine(ys, dest, h, rec, final_norm)
    return out.reshape(batch, seq, d)
```

```python
import functools
import math

import jax
import jax.numpy as jnp
from jax import lax
from jax.experimental import pallas as pl
from jax.experimental.pallas import tpu as pltpu

F32 = jnp.float32
BF16 = jnp.bfloat16

D_MODEL = 2048
GLA_HEADS = 8
GLA_DK = 64
GLA_DV = 128
GLA_GATE_RANK = 16
GLA_GATE_TEMP = 16.0
GDN_HEADS = 8
GDN_DK = 128
GDN_DV = 128
CONV_WIDTH = 4
CHUNK = 64
SUB = 16
SWA_Q_HEADS = 16
SWA_KV_HEADS = 2
SWA_HEAD_DIM = 64
WINDOW = 128
DIFF_HEADS = 8
DIFF_HEAD_DIM = 64
D_FF = 5632
N_EXPERTS = 8
TOP_K = 2
D_EXPERT = 7168
EPS = 1e-6
NEG = -1e30

LANE = 128
VMEM_LIMIT = 56 * 1024 * 1024

GLA_KD = GLA_HEADS * GLA_DK
GLA_VD = GLA_HEADS * GLA_DV
GDN_D = GDN_HEADS * GDN_DK
EVEN_MAIN = 2 * GLA_KD + 2 * GLA_VD + 4 * GDN_D
OFF_QA, OFF_KA, OFF_VA, OFF_GA = 0, GLA_KD, 2 * GLA_KD, 2 * GLA_KD + GLA_VD
OFF_QB = 2 * GLA_KD + 2 * GLA_VD
OFF_KB, OFF_VB, OFF_ZB = OFF_QB + GDN_D, OFF_QB + 2 * GDN_D, OFF_QB + 3 * GDN_D
MISC_RA, MISC_BB, MISC_AB = 0, GLA_GATE_RANK, GLA_GATE_RANK + GDN_HEADS

SWA_QD = SWA_Q_HEADS * SWA_HEAD_DIM
SWA_KVD = SWA_KV_HEADS * SWA_HEAD_DIM
DIFF_D = DIFF_HEADS * 2 * DIFF_HEAD_DIM
OFF_QC, OFF_KC, OFF_VC = 0, SWA_QD, SWA_QD + SWA_KVD
OFF_QD = SWA_QD + 2 * SWA_KVD
OFF_KD, OFF_VD = OFF_QD + DIFF_D, OFF_QD + 2 * DIFF_D
ODD_IN = OFF_VD + DIFF_D


def _cparams(sem):
    return pltpu.CompilerParams(dimension_semantics=sem, vmem_limit_bytes=VMEM_LIMIT)


def _dot(a, b):
    return jnp.dot(a, b, preferred_element_type=F32)


def _dot_nt(a, b):
    return lax.dot_general(a, b, (((1,), (1,)), ((), ())), preferred_element_type=F32)


def _dot_tn(a, b):
    return lax.dot_general(a, b, (((0,), (0,)), ((), ())), preferred_element_type=F32)


def _split2(x):
    hi = x.astype(BF16)
    lo = (x - hi.astype(F32)).astype(BF16)
    return hi, lo


def _dot_x3(a, b):
    ah, al = _split2(a)
    bh, bl = _split2(b)
    return _dot(ah, bh) + (_dot(ah, bl) + _dot(al, bh))


def _dot_exact_lhs(a_bf16, b):
    bh, bl = _split2(b)
    return _dot(a_bf16, bh) + _dot(a_bf16, bl)


def _sigmoid(x):
    return 1.0 / (1.0 + jnp.exp(-x))


def _silu(x):
    return x * _sigmoid(x)


def _rms(x, w):
    return x * lax.rsqrt(jnp.mean(x * x, axis=-1, keepdims=True) + EPS) * w


def _iota(shape, dim):
    return lax.broadcasted_iota(jnp.int32, shape, dim)


def _mm_body(*refs, n_parts, has_norm, has_res):
    x_refs = refs[:n_parts]
    pos = n_parts
    nw_ref = None
    if has_norm:
        nw_ref = refs[pos]
        pos += 1
    w_refs = refs[pos:pos + n_parts]
    pos += n_parts
    res_ref = None
    if has_res:
        res_ref = refs[pos]
        pos += 1
    o_ref = refs[pos]
    xs_ref = refs[pos + 1]

    @pl.when(pl.program_id(1) == 0)
    def _():
        for p in range(n_parts):
            x = x_refs[p][...].astype(F32)
            if has_norm:
                x = _rms(x, nw_ref[...])
            xs_ref[p] = x.astype(BF16)

    acc = _dot(xs_ref[0], w_refs[0][...].astype(BF16))
    for p in range(1, n_parts):
        acc = acc + _dot(xs_ref[p], w_refs[p][...].astype(BF16))
    if has_res:
        acc = acc + res_ref[...]
    o_ref[...] = acc.astype(o_ref.dtype)


def _matmul(xs, w, *, norm_w=None, res=None, out_dtype=F32, tm=512, tn=512):
    n_parts = len(xs)
    m, kp = xs[0].shape
    n = w.shape[1]
    assert w.shape[0] == kp * n_parts and m % tm == 0 and n % tn == 0
    in_specs = [pl.BlockSpec((tm, kp), lambda i, j: (i, 0)) for _ in range(n_parts)]
    args = list(xs)
    if norm_w is not None:
        assert n_parts == 1
        in_specs.append(pl.BlockSpec((1, kp), lambda i, j: (0, 0)))
        args.append(norm_w.reshape(1, kp))
    for p in range(n_parts):
        in_specs.append(pl.BlockSpec((kp, tn), lambda i, j, p=p: (p, j)))
        args.append(w)
    if res is not None:
        in_specs.append(pl.BlockSpec((tm, tn), lambda i, j: (i, j)))
        args.append(res)
    body = functools.partial(_mm_body, n_parts=n_parts, has_norm=norm_w is not None,
                             has_res=res is not None)
    return pl.pallas_call(
        body,
        grid=(m // tm, n // tn),
        in_specs=in_specs,
        out_specs=pl.BlockSpec((tm, tn), lambda i, j: (i, j)),
        out_shape=jax.ShapeDtypeStruct((m, n), out_dtype),
        scratch_shapes=[pltpu.VMEM((n_parts, tm, kp), BF16)],
        compiler_params=_cparams(("parallel", "arbitrary")),
        name="matmul",
    )(*args)


def _swiglu_step(xs, wg_ref, wu_ref, wd_ref, acc_ref):
    g = _dot(xs, wg_ref[...].astype(BF16))
    u = _dot(xs, wu_ref[...].astype(BF16))
    hmid = (_silu(g) * u).astype(BF16)
    acc_ref[...] += _dot(hmid, wd_ref[...].astype(BF16))


def _ffn_body(x_ref, nw_ref, wg_ref, wu_ref, wd_ref, o_ref, xs_ref, acc_ref):
    j = pl.program_id(1)

    @pl.when(j == 0)
    def _():
        xs_ref[...] = _rms(x_ref[...], nw_ref[...]).astype(BF16)
        acc_ref[...] = jnp.zeros_like(acc_ref)

    _swiglu_step(xs_ref[...], wg_ref, wu_ref, wd_ref, acc_ref)

    @pl.when(j == pl.num_programs(1) - 1)
    def _():
        o_ref[...] = x_ref[...] + acc_ref[...]


def _ffn(h, norm_w, w_gate, w_up, w_down, *, tm=512, tf=512):
    m, d = h.shape
    f = w_gate.shape[1]
    assert m % tm == 0 and f % tf == 0
    return pl.pallas_call(
        _ffn_body,
        grid=(m // tm, f // tf),
        in_specs=[
            pl.BlockSpec((tm, d), lambda i, j: (i, 0)),
            pl.BlockSpec((1, d), lambda i, j: (0, 0)),
            pl.BlockSpec((d, tf), lambda i, j: (0, j)),
            pl.BlockSpec((d, tf), lambda i, j: (0, j)),
            pl.BlockSpec((tf, d), lambda i, j: (j, 0)),
        ],
        out_specs=pl.BlockSpec((tm, d), lambda i, j: (i, 0)),
        out_shape=jax.ShapeDtypeStruct((m, d), F32),
        scratch_shapes=[pltpu.VMEM((tm, d), BF16), pltpu.VMEM((tm, d), F32)],
        compiler_params=_cparams(("parallel", "arbitrary")),
        name="ffn",
    )(h, norm_w.reshape(1, d), w_gate, w_up, w_down)


def _moe_body(te_ref, nu_ref, x_ref, wg_ref, wu_ref, wd_ref, o_ref, xs_ref, acc_ref):
    i = pl.program_id(0)
    j = pl.program_id(1)
    used = i < nu_ref[0]

    @pl.when(used & (j == 0))
    def _():
        xs_ref[...] = x_ref[...].astype(BF16)
        acc_ref[...] = jnp.zeros_like(acc_ref)

    @pl.when(used)
    def _():
        _swiglu_step(xs_ref[...], wg_ref, wu_ref, wd_ref, acc_ref)

    @pl.when(used & (j == pl.num_programs(1) - 1))
    def _():
        o_ref[...] = acc_ref[...]

    @pl.when(jnp.logical_not(used) & (j == 0))
    def _():
        o_ref[...] = jnp.zeros_like(o_ref)


def _moe_experts(xs, tile_expert, n_used, w_gate, w_up, w_down, *, tm, tf):
    p, d = xs.shape
    f = w_gate.shape[2]
    n_tiles = p // tm
    nj = f // tf

    def row_map(i, j, te, nu):
        return (jnp.minimum(i, nu[0] - 1), 0)

    def _eff(i, j, te, nu):
        used = i < nu[0]
        ie = jnp.minimum(i, nu[0] - 1)
        je = jnp.where(used, j, nj - 1)
        return te[ie], je

    def gu_map(i, j, te, nu):
        e, je = _eff(i, j, te, nu)
        return (e, 0, je)

    def dn_map(i, j, te, nu):
        e, je = _eff(i, j, te, nu)
        return (e, je, 0)

    grid_spec = pltpu.PrefetchScalarGridSpec(
        num_scalar_prefetch=2,
        grid=(n_tiles, nj),
        in_specs=[
            pl.BlockSpec((tm, d), row_map),
            pl.BlockSpec((None, d, tf), gu_map),
            pl.BlockSpec((None, d, tf), gu_map),
            pl.BlockSpec((None, tf, d), dn_map),
        ],
        out_specs=pl.BlockSpec((tm, d), lambda i, j, te, nu: (i, 0)),
        scratch_shapes=[pltpu.VMEM((tm, d), BF16), pltpu.VMEM((tm, d), F32)],
    )
    return pl.pallas_call(
        _moe_body,
        grid_spec=grid_spec,
        out_shape=jax.ShapeDtypeStruct((p, d), F32),
        compiler_params=_cparams(("arbitrary", "arbitrary")),
        name="moe_experts",
    )(tile_expert, n_used, xs, w_gate, w_up, w_down)


def _gla_body(q_ref, k_ref, v_ref, g_ref, misc_ref, wg_ref, bg_ref, nw_ref, o_ref, st_ref, *, tb):
    @pl.when(pl.program_id(2) == 0)
    def _():
        st_ref[...] = jnp.zeros_like(st_ref)

    c = CHUNK
    two_dk = 2 * GLA_DK
    ra = misc_ref[:, MISC_RA:MISC_RA + GLA_GATE_RANK]
    z = _dot_x3(ra, wg_ref[...]) + bg_ref[...]
    log_a = (jnp.minimum(z, 0.0) - jnp.log(1.0 + jnp.exp(-jnp.abs(z)))) * (1.0 / GLA_GATE_TEMP)

    row = _iota((c, c), 0)
    col = _iota((c, c), 1)
    tril = (row >= col).astype(BF16)
    rblk, cblk = row >> 4, col >> 4
    lane = _iota((1, two_dk), 1)
    head_mask = [(lane < GLA_DK).astype(F32), (lane >= GLA_DK).astype(F32)]
    st_mask = ((_iota((2 * GLA_DV, two_dk), 0) >= GLA_DV) == (_iota((2 * GLA_DV, two_dk), 1) >= GLA_DK))
    nsub = c // SUB

    for ci in range(tb // c):
        rows = slice(ci * c, (ci + 1) * c)
        q = q_ref[rows, :] * (GLA_DK ** -0.5)
        k = k_ref[rows, :]
        v = v_ref[rows, :]
        b = _dot_exact_lhs(tril, log_a[rows, :])
        b4 = b.reshape(nsub, SUB, two_dk)
        bend4 = jnp.broadcast_to(b4[:, SUB - 1:SUB, :], b4.shape)
        k_hat = k * jnp.exp(bend4.reshape(c, two_dk) - b)
        b_end = b[c - 1:c, :]
        st = st_ref[...]

        lhs_parts = []
        for jb in range(nsub - 1):
            bj = b[jb * SUB + SUB - 1:jb * SUB + SUB, :]
            lhs_parts.append(q * jnp.exp(jnp.minimum(b - bj, 0.0)))
        lhs_all = jnp.concatenate(lhs_parts, axis=0)

        q4 = q.reshape(nsub, SUB, two_dk)
        k4 = k.reshape(nsub, SUB, two_dk)
        diag = [jnp.zeros((c, c), F32), jnp.zeros((c, c), F32)]
        cloc = col - (rblk << 4)
        for jj in range(SUB):
            kj = k4[:, jj:jj + 1, :]
            bj = b4[:, jj:jj + 1, :]
            prod = (q4 * kj * jnp.exp(jnp.minimum(b4 - bj, 0.0))).reshape(c, two_dk)
            for hh in range(2):
                colv = jnp.sum(prod * head_mask[hh], axis=-1, keepdims=True)
                diag[hh] = jnp.where(cloc == jj, colv, diag[hh])

        o_inter = _dot_nt((q * jnp.exp(b)).astype(BF16), st.astype(BF16))
        outs = []
        for hh in range(2):
            off = _dot_nt((lhs_all * head_mask[hh]).astype(BF16), k_hat.astype(BF16))
            s_off = jnp.where(cblk == 0, off[0:c], jnp.where(cblk == 1, off[c:2 * c], off[2 * c:3 * c]))
            scores = jnp.where(rblk > cblk, s_off, jnp.where((rblk == cblk) & (row >= col), diag[hh], 0.0))
            vh = v[:, hh * GLA_DV:(hh + 1) * GLA_DV]
            o = _dot(scores.astype(BF16), vh.astype(BF16)) + o_inter[:, hh * GLA_DV:(hh + 1) * GLA_DV]
            gate = _silu(g_ref[rows, hh * GLA_DV:(hh + 1) * GLA_DV])
            outs.append(_rms(o, nw_ref[...]) * gate)
        o_ref[rows, :] = jnp.concatenate(outs, axis=1).astype(o_ref.dtype)

        k_dec = k * jnp.exp(b_end - b)
        upd = _dot_tn(v.astype(BF16), k_dec.astype(BF16))
        st_ref[...] = st * jnp.exp(b_end) + jnp.where(st_mask, upd, 0.0)


def _gla(p_main, p_misc, w_gate_up, b_gate, norm_w, *, batch, seq, tb=256):
    m = batch * seq
    nt = seq // tb
    hp = GLA_HEADS // 2
    two_dk, two_dv = 2 * GLA_DK, 2 * GLA_DV

    def rows(b, h, t):
        return b * nt + t

    body = functools.partial(_gla_body, tb=tb)
    return pl.pallas_call(
        body,
        grid=(batch, hp, nt),
        in_specs=[
            pl.BlockSpec((tb, two_dk), lambda b, h, t: (rows(b, h, t), OFF_QA // two_dk + h)),
            pl.BlockSpec((tb, two_dk), lambda b, h, t: (rows(b, h, t), OFF_KA // two_dk + h)),
            pl.BlockSpec((tb, two_dv), lambda b, h, t: (rows(b, h, t), OFF_VA // two_dv + h)),
            pl.BlockSpec((tb, two_dv), lambda b, h, t: (rows(b, h, t), OFF_GA // two_dv + h)),
            pl.BlockSpec((tb, LANE), lambda b, h, t: (rows(b, h, t), 0)),
            pl.BlockSpec((GLA_GATE_RANK, two_dk), lambda b, h, t: (0, h)),
            pl.BlockSpec((1, two_dk), lambda b, h, t: (0, h)),
            pl.BlockSpec((1, GLA_DV), lambda b, h, t: (0, 0)),
        ],
        out_specs=pl.BlockSpec((tb, two_dv), lambda b, h, t: (rows(b, h, t), h)),
        out_shape=jax.ShapeDtypeStruct((m, GLA_VD), BF16),
        scratch_shapes=[pltpu.VMEM((two_dv, two_dk), F32)],
        compiler_params=_cparams(("parallel", "parallel", "arbitrary")),
        name="gla",
    )(p_main, p_main, p_main, p_main, p_misc, w_gate_up, b_gate.reshape(1, -1), norm_w.reshape(1, -1))


GDN_GROUP = 4
GDN_ROWS = GDN_GROUP * CHUNK
GDN_PAD = 8


def _gdn_body(q_ref, k_ref, v_ref, z_ref, misc_ref, cw_ref, alog_ref, dtb_ref, nw_ref,
              o_ref, xbuf_ref, st_ref, *, nb):
    c = CHUNK
    hw = GDN_HEADS * GDN_DK

    @pl.when(pl.program_id(1) == 0)
    def _():
        st_ref[...] = jnp.zeros_like(st_ref)
        xbuf_ref[:, 0:GDN_PAD, :] = jnp.zeros((nb, GDN_PAD, 3 * hw), F32)

    def l2n(x):
        return x * lax.rsqrt(jnp.sum(x * x, axis=-1, keepdims=True) + EPS)

    row64 = _iota((c, c), 0)
    col64 = _iota((c, c), 1)
    tril64 = (row64 >= col64).astype(BF16)
    n = GDN_ROWS
    row = _iota((n, n), 0)
    col = _iota((n, n), 1)
    same_head = (row >> 6) == (col >> 6)
    eye = jnp.where(row == col, 1.0, 0.0)
    cw = cw_ref[...]

    for bi in range(nb):
        xbuf_ref[bi, GDN_PAD:GDN_PAD + c, 0:hw] = q_ref[bi]
        xbuf_ref[bi, GDN_PAD:GDN_PAD + c, hw:2 * hw] = k_ref[bi]
        xbuf_ref[bi, GDN_PAD:GDN_PAD + c, 2 * hw:3 * hw] = v_ref[bi]
        conv = jnp.zeros((c, 3 * hw), F32)
        for i in range(CONV_WIDTH):
            start = GDN_PAD - (CONV_WIDTH - 1) + i
            conv = conv + xbuf_ref[bi, start:start + c, :] * cw[i:i + 1, :]
        xbuf_ref[bi, 0:GDN_PAD, :] = xbuf_ref[bi, c:c + GDN_PAD, :]
        conv = _silu(conv)

        misc = misc_ref[bi]
        beta_all = _sigmoid(misc)
        sp_in = misc + dtb_ref[...]
        softplus = jnp.maximum(sp_in, 0.0) + jnp.log(1.0 + jnp.exp(-jnp.abs(sp_in)))
        log_g_all = -jnp.exp(alog_ref[...]) * softplus
        gam_all = _dot_exact_lhs(tril64, log_g_all)

        for grp in range(GDN_HEADS // GDN_GROUP):
            q_parts, k_parts, v_parts, beta_parts, gam_parts, gend_parts = [], [], [], [], [], []
            for h in range(GDN_GROUP):
                head = grp * GDN_GROUP + h
                q_parts.append(l2n(conv[:, head * GDN_DK:(head + 1) * GDN_DK]) * (GDN_DK ** -0.5))
                k_parts.append(l2n(conv[:, hw + head * GDN_DK:hw + (head + 1) * GDN_DK]))
                v_parts.append(conv[:, 2 * hw + head * GDN_DV:2 * hw + (head + 1) * GDN_DV])
                beta_parts.append(beta_all[:, MISC_BB + head:MISC_BB + head + 1])
                gcol = gam_all[:, MISC_AB + head:MISC_AB + head + 1]
                gam_parts.append(gcol)
                gend_parts.append(jnp.broadcast_to(gcol[c - 1:c, :], (c, 1)))
            qst = jnp.concatenate(q_parts, axis=0)
            kst = jnp.concatenate(k_parts, axis=0)
            vst = jnp.concatenate(v_parts, axis=0)
            beta = jnp.concatenate(beta_parts, axis=0)
            gam = jnp.concatenate(gam_parts, axis=0)
            gend = jnp.concatenate(gend_parts, axis=0)

            gam_b = jnp.broadcast_to(gam, (n, n))
            decay = jnp.exp(jnp.minimum(gam_b - gam_b.T, 0.0))
            kb = kst.astype(BF16)
            gram = _dot_nt(kb, kb)
            a_mat = jnp.where(same_head & (row > col), beta * gram * decay, 0.0)

            inv = eye - jnp.where(((row ^ col) == 1) & (row > col), a_mat, 0.0)
            for lg in range(1, 6):
                s = 1 << lg
                sel = ((row >> (lg + 1)) == (col >> (lg + 1))) & ((row & s) != 0) & ((col & s) == 0)
                a_s = jnp.where(sel, a_mat, 0.0)
                inv = inv - _dot_x3(_dot_x3(inv, a_s), inv)

            rk = (beta * jnp.exp(gam)) * kst
            rv = beta * vst
            wu = _dot_x3(inv, jnp.concatenate([rk, rv], axis=1))
            w_mat, u0 = wu[:, :GDN_DK], wu[:, GDN_DK:]
            qk = jnp.where(same_head & (row >= col), _dot_nt(qst.astype(BF16), kb) * decay, 0.0)
            qg = (qst * jnp.exp(gam)).astype(BF16)
            kd = (kst * jnp.exp(gend - gam)).astype(BF16)
            ge = jnp.exp(gend)

            u_parts, oi_parts = [], []
            for h in range(GDN_GROUP):
                slot = bi * GDN_HEADS + grp * GDN_GROUP + h
                r = slice(h * c, (h + 1) * c)
                st = st_ref[slot]
                stb = st.astype(BF16)
                u = u0[r] - _dot_nt(w_mat[r].astype(BF16), stb)
                oi_parts.append(_dot_nt(qg[r], stb))
                st_ref[slot] = st * ge[h * c:h * c + 1, :] + _dot_tn(u.astype(BF16), kd[r])
                u_parts.append(u)
            ust = jnp.concatenate(u_parts, axis=0)
            o = jnp.concatenate(oi_parts, axis=0) + _dot(qk.astype(BF16), ust.astype(BF16))

            for h in range(GDN_GROUP):
                head = grp * GDN_GROUP + h
                lanes = slice(head * GDN_DV, (head + 1) * GDN_DV)
                oh = _rms(o[h * c:(h + 1) * c], nw_ref[...]) * _silu(z_ref[bi, :, lanes])
                o_ref[bi, :, lanes] = oh.astype(o_ref.dtype)


def _gdn(p_main, p_misc, conv_w, a_log, dt_bias, norm_w, *, batch, seq, nb=1):
    nc = seq // CHUNK
    hw = GDN_HEADS * GDN_DK
    assert batch % nb == 0
    pm3 = p_main.reshape(batch, seq, p_main.shape[1])
    misc3 = p_misc.reshape(batch, seq, LANE)

    alog_l = jnp.zeros((1, LANE), F32).at[0, MISC_AB:MISC_AB + GDN_HEADS].set(a_log)
    dtb_l = jnp.zeros((1, LANE), F32).at[0, MISC_AB:MISC_AB + GDN_HEADS].set(dt_bias)
    body = functools.partial(_gdn_body, nb=nb)
    out = pl.pallas_call(
        body,
        grid=(batch // nb, nc),
        in_specs=[
            pl.BlockSpec((nb, CHUNK, hw), lambda b, t: (b, t, OFF_QB // hw)),
            pl.BlockSpec((nb, CHUNK, hw), lambda b, t: (b, t, OFF_KB // hw)),
            pl.BlockSpec((nb, CHUNK, hw), lambda b, t: (b, t, OFF_VB // hw)),
            pl.BlockSpec((nb, CHUNK, hw), lambda b, t: (b, t, OFF_ZB // hw)),
            pl.BlockSpec((nb, CHUNK, LANE), lambda b, t: (b, t, 0)),
            pl.BlockSpec((CONV_WIDTH, 3 * hw), lambda b, t: (0, 0)),
            pl.BlockSpec((1, LANE), lambda b, t: (0, 0)),
            pl.BlockSpec((1, LANE), lambda b, t: (0, 0)),
            pl.BlockSpec((1, GDN_DV), lambda b, t: (0, 0)),
        ],
        out_specs=pl.BlockSpec((nb, CHUNK, hw), lambda b, t: (b, t, 0)),
        out_shape=jax.ShapeDtypeStruct((batch, seq, GDN_D), BF16),
        scratch_shapes=[pltpu.VMEM((nb, CHUNK + GDN_PAD, 3 * hw), F32),
                        pltpu.VMEM((nb * GDN_HEADS, GDN_DV, GDN_DK), F32)],
        compiler_params=_cparams(("parallel", "arbitrary")),
        name="gdn",
    )(pm3, pm3, pm3, pm3, misc3, conv_w, alog_l, dtb_l, norm_w.reshape(1, -1))
    return out.reshape(batch * seq, GDN_D)


def _swa_body(sink_ref, q_ref, kp_ref, kc_ref, vp_ref, vc_ref, o_ref):
    w = WINDOW
    n = pl.program_id(1)
    g = SWA_Q_HEADS // SWA_KV_HEADS
    kk = jnp.concatenate([kp_ref[...], kc_ref[...]], axis=0)
    vv = jnp.concatenate([vp_ref[...], vc_ref[...]], axis=0)
    lane = _iota((1, LANE), 1)
    qpos = _iota((w, 2 * w), 0) + w
    kpos = _iota((w, 2 * w), 1)
    mask = (kpos <= qpos) & (kpos > qpos - w) & ((kpos >= w) | (n > 0))
    scale = SWA_HEAD_DIM ** -0.5
    for hk in range(SWA_KV_HEADS):
        in_head = (lane >= hk * SWA_HEAD_DIM) & (lane < (hk + 1) * SWA_HEAD_DIM)
        k_same = jnp.where(in_head, kk, 0.0)
        v_same = jnp.where(in_head, vv, 0.0)
        k_other = pltpu.roll(k_same, SWA_HEAD_DIM, 1)
        v_other = pltpu.roll(v_same, SWA_HEAD_DIM, 1)
        k_half = [k_same, k_other] if hk == 0 else [k_other, k_same]
        v_half = [v_same, v_other] if hk == 0 else [v_other, v_same]
        for pair in range(g // 2):
            blk = hk * (g // 2) + pair
            q2 = (q_ref[:, blk * LANE:(blk + 1) * LANE] * scale).astype(BF16)
            acc = jnp.zeros((w, LANE), F32)
            for half in range(2):
                head = 2 * blk + half
                s = _dot_nt(q2, k_half[half].astype(BF16))
                s = jnp.where(mask, s, NEG)
                sink = sink_ref[head]
                mx = jnp.maximum(jnp.max(s, axis=-1, keepdims=True), sink)
                p = jnp.exp(s - mx)
                denom = jnp.sum(p, axis=-1, keepdims=True) + jnp.exp(sink - mx)
                p = p / denom
                acc = acc + _dot(p.astype(BF16), v_half[half].astype(BF16))
            o_ref[:, blk * LANE:(blk + 1) * LANE] = acc.astype(o_ref.dtype)


def _swa(p2, sinks, *, batch, seq):
    m = batch * seq
    nb = seq // WINDOW

    def cur(b, n):
        return b * nb + n

    def prev(b, n):
        return b * nb + jnp.maximum(n - 1, 0)

    kcol = OFF_KC // LANE
    vcol = OFF_VC // LANE
    return pl.pallas_call(
        _swa_body,
        grid=(batch, nb),
        in_specs=[
            pl.BlockSpec(memory_space=pltpu.SMEM),
            pl.BlockSpec((WINDOW, SWA_QD), lambda b, n: (cur(b, n), 0)),
            pl.BlockSpec((WINDOW, LANE), lambda b, n: (prev(b, n), kcol)),
            pl.BlockSpec((WINDOW, LANE), lambda b, n: (cur(b, n), kcol)),
            pl.BlockSpec((WINDOW, LANE), lambda b, n: (prev(b, n), vcol)),
            pl.BlockSpec((WINDOW, LANE), lambda b, n: (cur(b, n), vcol)),
        ],
        out_specs=pl.BlockSpec((WINDOW, SWA_QD), lambda b, n: (cur(b, n), 0)),
        out_shape=jax.ShapeDtypeStruct((m, SWA_QD), BF16),
        compiler_params=_cparams(("parallel", "arbitrary")),
        name="swa",
    )(sinks, p2, p2, p2, p2, p2)


def _diff_body(q_ref, k_ref, v_ref, lamp_ref, nw_ref, o_ref, *, seq, tq, lambda_init):
    dh = DIFF_HEAD_DIM
    lane = _iota((1, 2 * dh), 1)
    lp = lamp_ref[...]
    lam = (jnp.exp(jnp.sum(lp[0:1] * lp[1:2], axis=-1, keepdims=True))
           - jnp.exp(jnp.sum(lp[2:3] * lp[3:4], axis=-1, keepdims=True)) + lambda_init)
    on_diag = _iota((tq, tq), 1) <= _iota((tq, tq), 0)
    kb = k_ref[...].astype(BF16)
    vb = v_ref[...].astype(BF16)
    for qi in range(seq // tq):
        lo = qi * tq
        q = q_ref[lo:lo + tq, :] * (dh ** -0.5)
        outs = []
        for half in range(2):
            in_half = (lane < dh) if half == 0 else (lane >= dh)
            qh = jnp.where(in_half, q, 0.0).astype(BF16)
            s_d = jnp.where(on_diag, _dot_nt(qh, kb[lo:lo + tq]), NEG)
            mx = jnp.max(s_d, axis=-1, keepdims=True)
            if qi > 0:
                s_o = _dot_nt(qh, kb[:lo])
                mx = jnp.maximum(mx, jnp.max(s_o, axis=-1, keepdims=True))
            p_d = jnp.exp(s_d - mx)
            den = jnp.sum(p_d, axis=-1, keepdims=True)
            o = _dot(p_d.astype(BF16), vb[lo:lo + tq])
            if qi > 0:
                p_o = jnp.exp(s_o - mx)
                den = den + jnp.sum(p_o, axis=-1, keepdims=True)
                o = o + _dot(p_o.astype(BF16), vb[:lo])
            outs.append(o / den)
        o = outs[0] - lam * outs[1]
        o_ref[lo:lo + tq, :] = (_rms(o, nw_ref[...]) * (1.0 - lambda_init)).astype(o_ref.dtype)


def _diff_attn(p2, lam_params, subln_w, lambda_init, *, batch, seq, tq=256):
    m = batch * seq
    hd = 2 * DIFF_HEAD_DIM
    body = functools.partial(_diff_body, seq=seq, tq=tq, lambda_init=lambda_init)
    return pl.pallas_call(
        body,
        grid=(batch, DIFF_HEADS),
        in_specs=[
            pl.BlockSpec((seq, hd), lambda b, h: (b, OFF_QD // hd + h)),
            pl.BlockSpec((seq, hd), lambda b, h: (b, OFF_KD // hd + h)),
            pl.BlockSpec((seq, hd), lambda b, h: (b, OFF_VD // hd + h)),
            pl.BlockSpec((4, DIFF_HEAD_DIM), lambda b, h: (0, 0)),
            pl.BlockSpec((1, hd), lambda b, h: (0, 0)),
        ],
        out_specs=pl.BlockSpec((seq, hd), lambda b, h: (b, h)),
        out_shape=jax.ShapeDtypeStruct((m, DIFF_D), BF16),
        compiler_params=_cparams(("parallel", "parallel")),
        name="diff_attn",
    )(p2, p2, p2, lam_params, subln_w.reshape(1, hd))


R_I1, R_I2, R_G1, R_G2 = 0, 1, 2, 3


def _router_body(h_ref, nw_ref, wr_ref, xn_ref, rec_ref):
    xn = _rms(h_ref[...], nw_ref[...])
    xn_ref[...] = xn
    logits = _dot_x3(xn, wr_ref[...])
    lane = _iota(logits.shape, 1)
    lg = jnp.where(lane < N_EXPERTS, logits, NEG)
    m1 = jnp.max(lg, axis=-1, keepdims=True)
    i1 = jnp.min(jnp.where(lg == m1, lane, LANE), axis=-1, keepdims=True)
    lg2 = jnp.where(lane == i1, NEG, lg)
    m2 = jnp.max(lg2, axis=-1, keepdims=True)
    i2 = jnp.min(jnp.where(lg2 == m2, lane, LANE), axis=-1, keepdims=True)
    e = jnp.exp(m2 - m1)
    g1 = 1.0 / (1.0 + e)
    g2 = e / (1.0 + e)
    rec = jnp.where(lane == R_I1, i1.astype(F32),
                    jnp.where(lane == R_I2, i2.astype(F32),
                              jnp.where(lane == R_G1, g1, jnp.where(lane == R_G2, g2, 0.0))))
    rec_ref[...] = rec


def _router(h, norm_w, w_router, *, tm=512):
    m, d = h.shape
    wr = jnp.zeros((d, LANE), F32).at[:, :N_EXPERTS].set(w_router)
    return pl.pallas_call(
        _router_body,
        grid=(m // tm,),
        in_specs=[
            pl.BlockSpec((tm, d), lambda i: (i, 0)),
            pl.BlockSpec((1, d), lambda i: (0, 0)),
            pl.BlockSpec((d, LANE), lambda i: (0, 0)),
        ],
        out_specs=[pl.BlockSpec((tm, d), lambda i: (i, 0)), pl.BlockSpec((tm, LANE), lambda i: (i, 0))],
        out_shape=[jax.ShapeDtypeStruct((m, d), F32), jax.ShapeDtypeStruct((m, LANE), F32)],
        compiler_params=_cparams(("parallel",)),
        name="router",
    )(h, norm_w.reshape(1, d), wr)


def _row_copy(src, s_row, dst, d_row, sem):
    return pltpu.make_async_copy(src.at[pl.ds(s_row, 1), :], dst.at[pl.ds(d_row, 1), :], sem)


def _scatter_body(dest_ref, x_ref, init_hbm, o_hbm, sem, *, rt):
    del init_hbm

    def issue(r, carry):
        for k in range(TOP_K):
            _row_copy(x_ref, r, o_hbm, dest_ref[0, 0, TOP_K * r + k], sem).start()
        return carry

    lax.fori_loop(0, rt, issue, 0)

    def drain(r, carry):
        for k in range(TOP_K):
            _row_copy(x_ref, 0, o_hbm, 0, sem).wait()
        return carry

    lax.fori_loop(0, rt, drain, 0)


def _scatter_rows(xn, dest, p_rows, *, rt=128):
    m, d = xn.shape
    dest3 = dest.reshape(m // rt, 1, TOP_K * rt)
    init = jnp.zeros((p_rows, d), F32)
    body = functools.partial(_scatter_body, rt=rt)
    return pl.pallas_call(
        body,
        grid=(m // rt,),
        in_specs=[
            pl.BlockSpec((1, 1, TOP_K * rt), lambda i: (i, 0, 0), memory_space=pltpu.SMEM),
            pl.BlockSpec((rt, d), lambda i: (i, 0)),
            pl.BlockSpec(memory_space=pl.ANY),
        ],
        out_specs=pl.BlockSpec(memory_space=pl.ANY),
        out_shape=jax.ShapeDtypeStruct((p_rows, d), F32),
        scratch_shapes=[pltpu.SemaphoreType.DMA(())],
        input_output_aliases={2: 0},
        compiler_params=_cparams(("arbitrary",)),
        name="moe_scatter",
    )(dest3, xn, init)


def _combine_body(dest_ref, y_hbm, h_ref, rec_ref, nw_ref, o_ref, buf_ref, sem, *, rt):
    def issue(r, carry):
        for k in range(TOP_K):
            _row_copy(y_hbm, dest_ref[0, 0, TOP_K * r + k], buf_ref.at[k], r, sem).start()
        return carry

    lax.fori_loop(0, rt, issue, 0)

    def drain(r, carry):
        for k in range(TOP_K):
            _row_copy(y_hbm, 0, buf_ref.at[k], 0, sem).wait()
        return carry

    lax.fori_loop(0, rt, drain, 0)
    rec = rec_ref[...]
    g1 = rec[:, R_G1:R_G1 + 1]
    g2 = rec[:, R_G2:R_G2 + 1]
    hn = h_ref[...] + g1 * buf_ref[0] + g2 * buf_ref[1]
    o_ref[...] = _rms(hn, nw_ref[...])


def _combine(ys, dest, h, rec, final_norm, *, rt=128):
    m, d = h.shape
    dest3 = dest.reshape(m // rt, 1, TOP_K * rt)
    body = functools.partial(_combine_body, rt=rt)
    return pl.pallas_call(
        body,
        grid=(m // rt,),
        in_specs=[
            pl.BlockSpec((1, 1, TOP_K * rt), lambda i: (i, 0, 0), memory_space=pltpu.SMEM),
            pl.BlockSpec(memory_space=pl.ANY),
            pl.BlockSpec((rt, d), lambda i: (i, 0)),
            pl.BlockSpec((rt, LANE), lambda i: (i, 0)),
            pl.BlockSpec((1, d), lambda i: (0, 0)),
        ],
        out_specs=pl.BlockSpec((rt, d), lambda i: (i, 0)),
        out_shape=jax.ShapeDtypeStruct((m, d), F32),
        scratch_shapes=[pltpu.VMEM((TOP_K, rt, d), F32), pltpu.SemaphoreType.DMA(())],
        compiler_params=_cparams(("arbitrary",)),
        name="moe_combine",
    )(dest3, ys, h, rec, final_norm.reshape(1, d))


def _routing_plan(rec, *, tm):
    m = rec.shape[0]
    idx = rec[:, R_I1:R_I2 + 1].astype(jnp.int32)
    flat = idx.reshape(-1)
    onehot = (flat[:, None] == jnp.arange(N_EXPERTS, dtype=jnp.int32)[None, :]).astype(jnp.int32)
    csum = jnp.cumsum(onehot, axis=0)
    rank = jnp.sum(onehot * csum, axis=1) - 1
    counts = csum[-1]
    padded = ((counts + tm - 1) // tm) * tm
    ends = jnp.cumsum(padded)
    starts = ends - padded
    dest = jnp.sum(onehot * starts[None, :], axis=1) + rank
    n_tiles = (TOP_K * m) // tm + N_EXPERTS
    tile_start = jnp.arange(n_tiles, dtype=jnp.int32) * tm
    tile_expert = jnp.minimum(jnp.sum((tile_start[:, None] >= ends[None, :]).astype(jnp.int32), axis=1),
                              N_EXPERTS - 1).astype(jnp.int32)
    n_used = (ends[-1] // tm).astype(jnp.int32).reshape(1)
    return dest.astype(jnp.int32), tile_expert, n_used, n_tiles * tm


def kernel(x, ev_norm_mix, ev_w_in, gla_w_gate_up, gla_b_gate, gla_norm_w, gdn_conv_w, gdn_a_log, gdn_dt_bias,
           gdn_norm_w, ev_w_out, ev_norm_ffn, ffn_w_gate, ffn_w_up, ffn_w_down,
           od_norm_mix, od_w_in, swa_sinks, diff_lambda_q1, diff_lambda_k1, diff_lambda_q2, diff_lambda_k2,
           diff_subln_w, od_w_out, od_norm_ffn, moe_w_router, moe_w_gate, moe_w_up, moe_w_down, final_norm):
    batch, seq, d = x.shape
    m = batch * seq
    h = x.reshape(m, d)

    w_in = ev_w_in[0]
    split = OFF_QB
    gdn_lo = split + GLA_GATE_RANK
    gdn_hi = gdn_lo + 4 * GDN_D
    w_main = jnp.concatenate([w_in[:, :split], w_in[:, gdn_lo:gdn_hi]], axis=1)
    w_misc = jnp.concatenate([w_in[:, split:gdn_lo], w_in[:, gdn_hi:],
                              jnp.zeros((d, LANE - GLA_GATE_RANK - 2 * GDN_HEADS), F32)], axis=1)
    p_main = _matmul((h,), w_main, norm_w=ev_norm_mix[0], out_dtype=F32, tm=512, tn=512)
    p_misc = _matmul((h,), w_misc, norm_w=ev_norm_mix[0], out_dtype=F32, tm=512, tn=LANE)
    oa = _gla(p_main, p_misc, gla_w_gate_up[0], gla_b_gate[0], gla_norm_w[0], batch=batch, seq=seq)
    ob = _gdn(p_main, p_misc, gdn_conv_w[0], gdn_a_log[0], gdn_dt_bias[0], gdn_norm_w[0], batch=batch, seq=seq)
    h = _matmul((oa, ob), ev_w_out[0], res=h, out_dtype=F32, tm=512, tn=512)
    h = _ffn(h, ev_norm_ffn[0], ffn_w_gate[0], ffn_w_up[0], ffn_w_down[0])

    lambda_init = 0.8 - 0.6 * math.exp(-0.3 * 1)
    p2 = _matmul((h,), od_w_in[0], norm_w=od_norm_mix[0], out_dtype=F32, tm=512, tn=256)
    oc = _swa(p2, swa_sinks[0], batch=batch, seq=seq)
    lam_params = jnp.stack([diff_lambda_q1[0], diff_lambda_k1[0], diff_lambda_q2[0], diff_lambda_k2[0]])
    od = _diff_attn(p2, lam_params, diff_subln_w[0], lambda_init, batch=batch, seq=seq)
    h = _matmul((oc, od), od_w_out[0], res=h, out_dtype=F32, tm=512, tn=512)

    moe_tm = 512
    xn, rec = _router(h, od_norm_ffn[0], moe_w_router[0])
    dest, tile_expert, n_used, p_rows = _routing_plan(rec, tm=moe_tm)
    xs = _scatter_rows(xn, dest, p_rows)
    ys = _moe_experts(xs, tile_expert, n_used, moe_w_gate[0], moe_w_up[0], moe_w_down[0], tm=moe_tm, tf=256)
    out = _combine(ys, dest, h, rec, final_norm)
    return out.reshape(batch, seq, d)
```

```python
import functools
import math

import jax
import jax.numpy as jnp
from jax import lax
from jax.experimental import pallas as pl
from jax.experimental.pallas import tpu as pltpu

F32 = jnp.float32
BF16 = jnp.bfloat16

D_MODEL = 2048
GLA_HEADS = 8
GLA_DK = 64
GLA_DV = 128
GLA_GATE_RANK = 16
GLA_GATE_TEMP = 16.0
GDN_HEADS = 8
GDN_DK = 128
GDN_DV = 128
CONV_WIDTH = 4
CHUNK = 64
SUB = 16
SWA_Q_HEADS = 16
SWA_KV_HEADS = 2
SWA_HEAD_DIM = 64
WINDOW = 128
DIFF_HEADS = 8
DIFF_HEAD_DIM = 64
D_FF = 5632
N_EXPERTS = 8
TOP_K = 2
D_EXPERT = 7168
EPS = 1e-6
NEG = -1e30

LANE = 128
VMEM_LIMIT = 56 * 1024 * 1024

GLA_KD = GLA_HEADS * GLA_DK
GLA_VD = GLA_HEADS * GLA_DV
GDN_D = GDN_HEADS * GDN_DK
EVEN_MAIN = 2 * GLA_KD + 2 * GLA_VD + 4 * GDN_D
OFF_QA, OFF_KA, OFF_VA, OFF_GA = 0, GLA_KD, 2 * GLA_KD, 2 * GLA_KD + GLA_VD
OFF_QB = 2 * GLA_KD + 2 * GLA_VD
OFF_KB, OFF_VB, OFF_ZB = OFF_QB + GDN_D, OFF_QB + 2 * GDN_D, OFF_QB + 3 * GDN_D
MISC_RA, MISC_BB, MISC_AB = 0, GLA_GATE_RANK, GLA_GATE_RANK + GDN_HEADS

SWA_QD = SWA_Q_HEADS * SWA_HEAD_DIM
SWA_KVD = SWA_KV_HEADS * SWA_HEAD_DIM
DIFF_D = DIFF_HEADS * 2 * DIFF_HEAD_DIM
OFF_QC, OFF_KC, OFF_VC = 0, SWA_QD, SWA_QD + SWA_KVD
OFF_QD = SWA_QD + 2 * SWA_KVD
OFF_KD, OFF_VD = OFF_QD + DIFF_D, OFF_QD + 2 * DIFF_D
ODD_IN = OFF_VD + DIFF_D


def _cparams(sem):
    return pltpu.CompilerParams(dimension_semantics=sem, vmem_limit_bytes=VMEM_LIMIT)


def _dot(a, b):
    return jnp.dot(a, b, preferred_element_type=F32)


def _dot_nt(a, b):
    return lax.dot_general(a, b, (((1,), (1,)), ((), ())), preferred_element_type=F32)


def _dot_tn(a, b):
    return lax.dot_general(a, b, (((0,), (0,)), ((), ())), preferred_element_type=F32)


def _split2(x):
    hi = x.astype(BF16)
    lo = (x - hi.astype(F32)).astype(BF16)
    return hi, lo


def _dot_x3(a, b):
    ah, al = _split2(a)
    bh, bl = _split2(b)
    return _dot(ah, bh) + (_dot(ah, bl) + _dot(al, bh))


def _dot_exact_lhs(a_bf16, b):
    bh, bl = _split2(b)
    return _dot(a_bf16, bh) + _dot(a_bf16, bl)


def _sigmoid(x):
    return 1.0 / (1.0 + jnp.exp(-x))


def _silu(x):
    return x * _sigmoid(x)


def _rms(x, w):
    return x * lax.rsqrt(jnp.mean(x * x, axis=-1, keepdims=True) + EPS) * w


def _iota(shape, dim):
    return lax.broadcasted_iota(jnp.int32, shape, dim)


def _mm_body(*refs, n_parts, has_norm, has_res):
    x_refs = refs[:n_parts]
    pos = n_parts
    nw_ref = None
    if has_norm:
        nw_ref = refs[pos]
        pos += 1
    w_refs = refs[pos:pos + n_parts]
    pos += n_parts
    res_ref = None
    if has_res:
        res_ref = refs[pos]
        pos += 1
    o_ref = refs[pos]
    xs_ref = refs[pos + 1]

    @pl.when(pl.program_id(1) == 0)
    def _():
        for p in range(n_parts):
            x = x_refs[p][...].astype(F32)
            if has_norm:
                x = _rms(x, nw_ref[...])
            xs_ref[p] = x.astype(BF16)

    acc = _dot(xs_ref[0], w_refs[0][...].astype(BF16))
    for p in range(1, n_parts):
        acc = acc + _dot(xs_ref[p], w_refs[p][...].astype(BF16))
    if has_res:
        acc = acc + res_ref[...]
    o_ref[...] = acc.astype(o_ref.dtype)


def _matmul(xs, w, *, norm_w=None, res=None, out_dtype=F32, tm=512, tn=512):
    n_parts = len(xs)
    m, kp = xs[0].shape
    n = w.shape[1]
    assert w.shape[0] == kp * n_parts and m % tm == 0 and n % tn == 0
    in_specs = [pl.BlockSpec((tm, kp), lambda i, j: (i, 0)) for _ in range(n_parts)]
    args = list(xs)
    if norm_w is not None:
        assert n_parts == 1
        in_specs.append(pl.BlockSpec((1, kp), lambda i, j: (0, 0)))
        args.append(norm_w.reshape(1, kp))
    for p in range(n_parts):
        in_specs.append(pl.BlockSpec((kp, tn), lambda i, j, p=p: (p, j)))
        args.append(w)
    if res is not None:
        in_specs.append(pl.BlockSpec((tm, tn), lambda i, j: (i, j)))
        args.append(res)
    body = functools.partial(_mm_body, n_parts=n_parts, has_norm=norm_w is not None,
                             has_res=res is not None)
    return pl.pallas_call(
        body,
        grid=(m // tm, n // tn),
        in_specs=in_specs,
        out_specs=pl.BlockSpec((tm, tn), lambda i, j: (i, j)),
        out_shape=jax.ShapeDtypeStruct((m, n), out_dtype),
        scratch_shapes=[pltpu.VMEM((n_parts, tm, kp), BF16)],
        compiler_params=_cparams(("parallel", "arbitrary")),
        name="matmul",
    )(*args)


def _swiglu_step(xs, wg_ref, wu_ref, wd_ref, acc_ref):
    g = _dot(xs, wg_ref[...].astype(BF16))
    u = _dot(xs, wu_ref[...].astype(BF16))
    hmid = (_silu(g) * u).astype(BF16)
    acc_ref[...] += _dot(hmid, wd_ref[...].astype(BF16))


def _ffn_body(x_ref, nw_ref, wg_ref, wu_ref, wd_ref, o_ref, xs_ref, acc_ref):
    j = pl.program_id(1)

    @pl.when(j == 0)
    def _():
        xs_ref[...] = _rms(x_ref[...], nw_ref[...]).astype(BF16)
        acc_ref[...] = jnp.zeros_like(acc_ref)

    _swiglu_step(xs_ref[...], wg_ref, wu_ref, wd_ref, acc_ref)

    @pl.when(j == pl.num_programs(1) - 1)
    def _():
        o_ref[...] = x_ref[...] + acc_ref[...]


def _ffn(h, norm_w, w_gate, w_up, w_down, *, tm=512, tf=512):
    m, d = h.shape
    f = w_gate.shape[1]
    assert m % tm == 0 and f % tf == 0
    return pl.pallas_call(
        _ffn_body,
        grid=(m // tm, f // tf),
        in_specs=[
            pl.BlockSpec((tm, d), lambda i, j: (i, 0)),
            pl.BlockSpec((1, d), lambda i, j: (0, 0)),
            pl.BlockSpec((d, tf), lambda i, j: (0, j)),
            pl.BlockSpec((d, tf), lambda i, j: (0, j)),
            pl.BlockSpec((tf, d), lambda i, j: (j, 0)),
        ],
        out_specs=pl.BlockSpec((tm, d), lambda i, j: (i, 0)),
        out_shape=jax.ShapeDtypeStruct((m, d), F32),
        scratch_shapes=[pltpu.VMEM((tm, d), BF16), pltpu.VMEM((tm, d), F32)],
        compiler_params=_cparams(("parallel", "arbitrary")),
        name="ffn",
    )(h, norm_w.reshape(1, d), w_gate, w_up, w_down)


def _moe_gu_body(te_ref, nu_ref, x_ref, wg_ref, wu_ref, o_ref, xs_ref):
    used = pl.program_id(0) < nu_ref[0]

    @pl.when(used & (pl.program_id(1) == 0))
    def _():
        xs_ref[...] = x_ref[...].astype(BF16)

    @pl.when(used)
    def _():
        xs = xs_ref[...]
        g = _dot(xs, wg_ref[...].astype(BF16))
        u = _dot(xs, wu_ref[...].astype(BF16))
        o_ref[...] = (_silu(g) * u).astype(o_ref.dtype)

    @pl.when(jnp.logical_not(used))
    def _():
        o_ref[...] = jnp.zeros_like(o_ref)


def _moe_dn_body(te_ref, nu_ref, h_ref, wd_ref, o_ref):
    used = pl.program_id(0) < nu_ref[0]

    @pl.when(used)
    def _():
        o_ref[...] = _dot(h_ref[...], wd_ref[...].astype(BF16))

    @pl.when(jnp.logical_not(used))
    def _():
        o_ref[...] = jnp.zeros_like(o_ref)


def _moe_experts(xs, tile_expert, n_used, w_gate, w_up, w_down, *, tm, tf, tn):
    p, d = xs.shape
    f = w_gate.shape[2]
    n_tiles = p // tm
    nj = f // tf
    nn = d // tn

    def row_map(i, j, te, nu):
        return (jnp.minimum(i, nu[0] - 1), 0)

    def w_idx(i, j, te, nu, last):
        ie = jnp.minimum(i, nu[0] - 1)
        return te[ie], jnp.where(i < nu[0], j, last)

    def gu_map(i, j, te, nu):
        e, je = w_idx(i, j, te, nu, nj - 1)
        return (e, 0, je)

    def dn_map(i, j, te, nu):
        e, je = w_idx(i, j, te, nu, nn - 1)
        return (e, 0, je)

    hidden = pl.pallas_call(
        _moe_gu_body,
        grid_spec=pltpu.PrefetchScalarGridSpec(
            num_scalar_prefetch=2,
            grid=(n_tiles, nj),
            in_specs=[
                pl.BlockSpec((tm, d), row_map),
                pl.BlockSpec((None, d, tf), gu_map),
                pl.BlockSpec((None, d, tf), gu_map),
            ],
            out_specs=pl.BlockSpec((tm, tf), lambda i, j, te, nu: (i, j)),
            scratch_shapes=[pltpu.VMEM((tm, d), BF16)],
        ),
        out_shape=jax.ShapeDtypeStruct((p, f), BF16),
        compiler_params=_cparams(("arbitrary", "arbitrary")),
        name="moe_gate_up",
    )(tile_expert, n_used, xs, w_gate, w_up)
    return pl.pallas_call(
        _moe_dn_body,
        grid_spec=pltpu.PrefetchScalarGridSpec(
            num_scalar_prefetch=2,
            grid=(n_tiles, nn),
            in_specs=[
                pl.BlockSpec((tm, f), row_map),
                pl.BlockSpec((None, f, tn), dn_map),
            ],
            out_specs=pl.BlockSpec((tm, tn), lambda i, j, te, nu: (i, j)),
        ),
        out_shape=jax.ShapeDtypeStruct((p, d), F32),
        compiler_params=_cparams(("arbitrary", "arbitrary")),
        name="moe_down",
    )(tile_expert, n_used, hidden, w_down)


def _gla_body(q_ref, k_ref, v_ref, g_ref, misc_ref, wg_ref, bg_ref, nw_ref, o_ref, st_ref, *, tb):
    @pl.when(pl.program_id(2) == 0)
    def _():
        st_ref[...] = jnp.zeros_like(st_ref)

    c = CHUNK
    two_dk = 2 * GLA_DK
    ra = misc_ref[:, MISC_RA:MISC_RA + GLA_GATE_RANK]
    z = _dot_x3(ra, wg_ref[...]) + bg_ref[...]
    log_a = (jnp.minimum(z, 0.0) - jnp.log(1.0 + jnp.exp(-jnp.abs(z)))) * (1.0 / GLA_GATE_TEMP)

    row = _iota((c, c), 0)
    col = _iota((c, c), 1)
    tril = (row >= col).astype(BF16)
    rblk, cblk = row >> 4, col >> 4
    lane = _iota((1, two_dk), 1)
    head_mask = [(lane < GLA_DK).astype(F32), (lane >= GLA_DK).astype(F32)]
    st_mask = ((_iota((2 * GLA_DV, two_dk), 0) >= GLA_DV) == (_iota((2 * GLA_DV, two_dk), 1) >= GLA_DK))
    nsub = c // SUB

    for ci in range(tb // c):
        rows = slice(ci * c, (ci + 1) * c)
        q = q_ref[rows, :] * (GLA_DK ** -0.5)
        k = k_ref[rows, :]
        v = v_ref[rows, :]
        b = _dot_exact_lhs(tril, log_a[rows, :])
        b4 = b.reshape(nsub, SUB, two_dk)
        bend4 = jnp.broadcast_to(b4[:, SUB - 1:SUB, :], b4.shape)
        k_hat = k * jnp.exp(bend4.reshape(c, two_dk) - b)
        b_end = b[c - 1:c, :]
        st = st_ref[...]

        lhs_parts = []
        for jb in range(nsub - 1):
            bj = b[jb * SUB + SUB - 1:jb * SUB + SUB, :]
            lhs_parts.append(q * jnp.exp(jnp.minimum(b - bj, 0.0)))
        lhs_all = jnp.concatenate(lhs_parts, axis=0)

        q4 = q.reshape(nsub, SUB, two_dk)
        k4 = k.reshape(nsub, SUB, two_dk)
        diag = [jnp.zeros((c, c), F32), jnp.zeros((c, c), F32)]
        cloc = col - (rblk << 4)
        for jj in range(SUB):
            kj = k4[:, jj:jj + 1, :]
            bj = b4[:, jj:jj + 1, :]
            prod = (q4 * kj * jnp.exp(jnp.minimum(b4 - bj, 0.0))).reshape(c, two_dk)
            for hh in range(2):
                colv = jnp.sum(prod * head_mask[hh], axis=-1, keepdims=True)
                diag[hh] = jnp.where(cloc == jj, colv, diag[hh])

        o_inter = _dot_nt((q * jnp.exp(b)).astype(BF16), st.astype(BF16))
        outs = []
        for hh in range(2):
            off = _dot_nt((lhs_all * head_mask[hh]).astype(BF16), k_hat.astype(BF16))
            s_off = jnp.where(cblk == 0, off[0:c], jnp.where(cblk == 1, off[c:2 * c], off[2 * c:3 * c]))
            scores = jnp.where(rblk > cblk, s_off, jnp.where((rblk == cblk) & (row >= col), diag[hh], 0.0))
            vh = v[:, hh * GLA_DV:(hh + 1) * GLA_DV]
            o = _dot(scores.astype(BF16), vh.astype(BF16)) + o_inter[:, hh * GLA_DV:(hh + 1) * GLA_DV]
            gate = _silu(g_ref[rows, hh * GLA_DV:(hh + 1) * GLA_DV])
            outs.append(_rms(o, nw_ref[...]) * gate)
        o_ref[rows, :] = jnp.concatenate(outs, axis=1).astype(o_ref.dtype)

        k_dec = k * jnp.exp(b_end - b)
        upd = _dot_tn(v.astype(BF16), k_dec.astype(BF16))
        st_ref[...] = st * jnp.exp(b_end) + jnp.where(st_mask, upd, 0.0)


def _gla(p_main, p_misc, w_gate_up, b_gate, norm_w, *, batch, seq, tb=256):
    m = batch * seq
    nt = seq // tb
    hp = GLA_HEADS // 2
    two_dk, two_dv = 2 * GLA_DK, 2 * GLA_DV

    def rows(b, h, t):
        return b * nt + t

    body = functools.partial(_gla_body, tb=tb)
    return pl.pallas_call(
        body,
        grid=(batch, hp, nt),
        in_specs=[
            pl.BlockSpec((tb, two_dk), lambda b, h, t: (rows(b, h, t), OFF_QA // two_dk + h)),
            pl.BlockSpec((tb, two_dk), lambda b, h, t: (rows(b, h, t), OFF_KA // two_dk + h)),
            pl.BlockSpec((tb, two_dv), lambda b, h, t: (rows(b, h, t), OFF_VA // two_dv + h)),
            pl.BlockSpec((tb, two_dv), lambda b, h, t: (rows(b, h, t), OFF_GA // two_dv + h)),
            pl.BlockSpec((tb, LANE), lambda b, h, t: (rows(b, h, t), 0)),
            pl.BlockSpec((GLA_GATE_RANK, two_dk), lambda b, h, t: (0, h)),
            pl.BlockSpec((1, two_dk), lambda b, h, t: (0, h)),
            pl.BlockSpec((1, GLA_DV), lambda b, h, t: (0, 0)),
        ],
        out_specs=pl.BlockSpec((tb, two_dv), lambda b, h, t: (rows(b, h, t), h)),
        out_shape=jax.ShapeDtypeStruct((m, GLA_VD), BF16),
        scratch_shapes=[pltpu.VMEM((two_dv, two_dk), F32)],
        compiler_params=_cparams(("parallel", "parallel", "arbitrary")),
        name="gla",
    )(p_main, p_main, p_main, p_main, p_misc, w_gate_up, b_gate.reshape(1, -1), norm_w.reshape(1, -1))


GDN_GROUP = 4
GDN_ROWS = GDN_GROUP * CHUNK
GDN_PAD = 8


def _gdn_body(q_ref, k_ref, v_ref, z_ref, misc_ref, cw_ref, alog_ref, dtb_ref, nw_ref,
              o_ref, xbuf_ref, st_ref, *, nb):
    c = CHUNK
    hw = GDN_HEADS * GDN_DK

    @pl.when(pl.program_id(1) == 0)
    def _():
        st_ref[...] = jnp.zeros_like(st_ref)
        xbuf_ref[:, 0:GDN_PAD, :] = jnp.zeros((nb, GDN_PAD, 3 * hw), F32)

    def l2n(x):
        return x * lax.rsqrt(jnp.sum(x * x, axis=-1, keepdims=True) + EPS)

    row64 = _iota((c, c), 0)
    col64 = _iota((c, c), 1)
    tril64 = (row64 >= col64).astype(BF16)
    n = GDN_ROWS
    row = _iota((n, n), 0)
    col = _iota((n, n), 1)
    same_head = (row >> 6) == (col >> 6)
    eye = jnp.where(row == col, 1.0, 0.0)
    cw = cw_ref[...]

    for bi in range(nb):
        xbuf_ref[bi, GDN_PAD:GDN_PAD + c, 0:hw] = q_ref[bi]
        xbuf_ref[bi, GDN_PAD:GDN_PAD + c, hw:2 * hw] = k_ref[bi]
        xbuf_ref[bi, GDN_PAD:GDN_PAD + c, 2 * hw:3 * hw] = v_ref[bi]
        conv = jnp.zeros((c, 3 * hw), F32)
        for i in range(CONV_WIDTH):
            start = GDN_PAD - (CONV_WIDTH - 1) + i
            conv = conv + xbuf_ref[bi, start:start + c, :] * cw[i:i + 1, :]
        xbuf_ref[bi, 0:GDN_PAD, :] = xbuf_ref[bi, c:c + GDN_PAD, :]
        conv = _silu(conv)

        misc = misc_ref[bi]
        beta_all = _sigmoid(misc)
        sp_in = misc + dtb_ref[...]
        softplus = jnp.maximum(sp_in, 0.0) + jnp.log(1.0 + jnp.exp(-jnp.abs(sp_in)))
        log_g_all = -jnp.exp(alog_ref[...]) * softplus
        gam_all = _dot_exact_lhs(tril64, log_g_all)

        for grp in range(GDN_HEADS // GDN_GROUP):
            q_parts, k_parts, v_parts, beta_parts, gam_parts, gend_parts = [], [], [], [], [], []
            for h in range(GDN_GROUP):
                head = grp * GDN_GROUP + h
                q_parts.append(l2n(conv[:, head * GDN_DK:(head + 1) * GDN_DK]) * (GDN_DK ** -0.5))
                k_parts.append(l2n(conv[:, hw + head * GDN_DK:hw + (head + 1) * GDN_DK]))
                v_parts.append(conv[:, 2 * hw + head * GDN_DV:2 * hw + (head + 1) * GDN_DV])
                beta_parts.append(beta_all[:, MISC_BB + head:MISC_BB + head + 1])
                gcol = gam_all[:, MISC_AB + head:MISC_AB + head + 1]
                gam_parts.append(gcol)
                gend_parts.append(jnp.broadcast_to(gcol[c - 1:c, :], (c, 1)))
            qst = jnp.concatenate(q_parts, axis=0)
            kst = jnp.concatenate(k_parts, axis=0)
            vst = jnp.concatenate(v_parts, axis=0)
            beta = jnp.concatenate(beta_parts, axis=0)
            gam = jnp.concatenate(gam_parts, axis=0)
            gend = jnp.concatenate(gend_parts, axis=0)

            gam_b = jnp.broadcast_to(gam, (n, n))
            decay = jnp.exp(jnp.minimum(gam_b - gam_b.T, 0.0))
            kb = kst.astype(BF16)
            gram = _dot_nt(kb, kb)
            a_mat = jnp.where(same_head & (row > col), beta * gram * decay, 0.0)

            inv = eye - jnp.where(((row ^ col) == 1) & (row > col), a_mat, 0.0)
            for lg in range(1, 6):
                s = 1 << lg
                sel = ((row >> (lg + 1)) == (col >> (lg + 1))) & ((row & s) != 0) & ((col & s) == 0)
                a_s = jnp.where(sel, a_mat, 0.0)
                inv = inv - _dot_x3(_dot_x3(inv, a_s), inv)

            rk = (beta * jnp.exp(gam)) * kst
            rv = beta * vst
            wu = _dot_x3(inv, jnp.concatenate([rk, rv], axis=1))
            w_mat, u0 = wu[:, :GDN_DK], wu[:, GDN_DK:]
            qk = jnp.where(same_head & (row >= col), _dot_nt(qst.astype(BF16), kb) * decay, 0.0)
            qg = (qst * jnp.exp(gam)).astype(BF16)
            kd = (kst * jnp.exp(gend - gam)).astype(BF16)
            ge = jnp.exp(gend)

            u_parts, oi_parts = [], []
            for h in range(GDN_GROUP):
                slot = bi * GDN_HEADS + grp * GDN_GROUP + h
                r = slice(h * c, (h + 1) * c)
                st = st_ref[slot]
                stb = st.astype(BF16)
                u = u0[r] - _dot_nt(w_mat[r].astype(BF16), stb)
                oi_parts.append(_dot_nt(qg[r], stb))
                st_ref[slot] = st * ge[h * c:h * c + 1, :] + _dot_tn(u.astype(BF16), kd[r])
                u_parts.append(u)
            ust = jnp.concatenate(u_parts, axis=0)
            o = jnp.concatenate(oi_parts, axis=0) + _dot(qk.astype(BF16), ust.astype(BF16))

            for h in range(GDN_GROUP):
                head = grp * GDN_GROUP + h
                lanes = slice(head * GDN_DV, (head + 1) * GDN_DV)
                oh = _rms(o[h * c:(h + 1) * c], nw_ref[...]) * _silu(z_ref[bi, :, lanes])
                o_ref[bi, :, lanes] = oh.astype(o_ref.dtype)


def _gdn(p_main, p_misc, conv_w, a_log, dt_bias, norm_w, *, batch, seq, nb=1):
    nc = seq // CHUNK
    hw = GDN_HEADS * GDN_DK
    assert batch % nb == 0
    pm3 = p_main.reshape(batch, seq, p_main.shape[1])
    misc3 = p_misc.reshape(batch, seq, LANE)

    alog_l = jnp.zeros((1, LANE), F32).at[0, MISC_AB:MISC_AB + GDN_HEADS].set(a_log)
    dtb_l = jnp.zeros((1, LANE), F32).at[0, MISC_AB:MISC_AB + GDN_HEADS].set(dt_bias)
    body = functools.partial(_gdn_body, nb=nb)
    out = pl.pallas_call(
        body,
        grid=(batch // nb, nc),
        in_specs=[
            pl.BlockSpec((nb, CHUNK, hw), lambda b, t: (b, t, OFF_QB // hw)),
            pl.BlockSpec((nb, CHUNK, hw), lambda b, t: (b, t, OFF_KB // hw)),
            pl.BlockSpec((nb, CHUNK, hw), lambda b, t: (b, t, OFF_VB // hw)),
            pl.BlockSpec((nb, CHUNK, hw), lambda b, t: (b, t, OFF_ZB // hw)),
            pl.BlockSpec((nb, CHUNK, LANE), lambda b, t: (b, t, 0)),
            pl.BlockSpec((CONV_WIDTH, 3 * hw), lambda b, t: (0, 0)),
            pl.BlockSpec((1, LANE), lambda b, t: (0, 0)),
            pl.BlockSpec((1, LANE), lambda b, t: (0, 0)),
            pl.BlockSpec((1, GDN_DV), lambda b, t: (0, 0)),
        ],
        out_specs=pl.BlockSpec((nb, CHUNK, hw), lambda b, t: (b, t, 0)),
        out_shape=jax.ShapeDtypeStruct((batch, seq, GDN_D), BF16),
        scratch_shapes=[pltpu.VMEM((nb, CHUNK + GDN_PAD, 3 * hw), F32),
                        pltpu.VMEM((nb * GDN_HEADS, GDN_DV, GDN_DK), F32)],
        compiler_params=_cparams(("parallel", "arbitrary")),
        name="gdn",
    )(pm3, pm3, pm3, pm3, misc3, conv_w, alog_l, dtb_l, norm_w.reshape(1, -1))
    return out.reshape(batch * seq, GDN_D)


def _swa_body(sink_ref, q_ref, kp_ref, kc_ref, vp_ref, vc_ref, o_ref):
    w = WINDOW
    n = pl.program_id(1)
    g = SWA_Q_HEADS // SWA_KV_HEADS
    kk = jnp.concatenate([kp_ref[...], kc_ref[...]], axis=0)
    vv = jnp.concatenate([vp_ref[...], vc_ref[...]], axis=0)
    lane = _iota((1, LANE), 1)
    qpos = _iota((w, 2 * w), 0) + w
    kpos = _iota((w, 2 * w), 1)
    mask = (kpos <= qpos) & (kpos > qpos - w) & ((kpos >= w) | (n > 0))
    scale = SWA_HEAD_DIM ** -0.5
    for hk in range(SWA_KV_HEADS):
        in_head = (lane >= hk * SWA_HEAD_DIM) & (lane < (hk + 1) * SWA_HEAD_DIM)
        k_same = jnp.where(in_head, kk, 0.0)
        v_same = jnp.where(in_head, vv, 0.0)
        k_other = pltpu.roll(k_same, SWA_HEAD_DIM, 1)
        v_other = pltpu.roll(v_same, SWA_HEAD_DIM, 1)
        k_half = [k_same, k_other] if hk == 0 else [k_other, k_same]
        v_half = [v_same, v_other] if hk == 0 else [v_other, v_same]
        for pair in range(g // 2):
            blk = hk * (g // 2) + pair
            q2 = (q_ref[:, blk * LANE:(blk + 1) * LANE] * scale).astype(BF16)
            acc = jnp.zeros((w, LANE), F32)
            for half in range(2):
                head = 2 * blk + half
                s = _dot_nt(q2, k_half[half].astype(BF16))
                s = jnp.where(mask, s, NEG)
                sink = sink_ref[head]
                mx = jnp.maximum(jnp.max(s, axis=-1, keepdims=True), sink)
                p = jnp.exp(s - mx)
                denom = jnp.sum(p, axis=-1, keepdims=True) + jnp.exp(sink - mx)
                p = p / denom
                acc = acc + _dot(p.astype(BF16), v_half[half].astype(BF16))
            o_ref[:, blk * LANE:(blk + 1) * LANE] = acc.astype(o_ref.dtype)


def _swa(p2, sinks, *, batch, seq):
    m = batch * seq
    nb = seq // WINDOW

    def cur(b, n):
        return b * nb + n

    def prev(b, n):
        return b * nb + jnp.maximum(n - 1, 0)

    kcol = OFF_KC // LANE
    vcol = OFF_VC // LANE
    return pl.pallas_call(
        _swa_body,
        grid=(batch, nb),
        in_specs=[
            pl.BlockSpec(memory_space=pltpu.SMEM),
            pl.BlockSpec((WINDOW, SWA_QD), lambda b, n: (cur(b, n), 0)),
            pl.BlockSpec((WINDOW, LANE), lambda b, n: (prev(b, n), kcol)),
            pl.BlockSpec((WINDOW, LANE), lambda b, n: (cur(b, n), kcol)),
            pl.BlockSpec((WINDOW, LANE), lambda b, n: (prev(b, n), vcol)),
            pl.BlockSpec((WINDOW, LANE), lambda b, n: (cur(b, n), vcol)),
        ],
        out_specs=pl.BlockSpec((WINDOW, SWA_QD), lambda b, n: (cur(b, n), 0)),
        out_shape=jax.ShapeDtypeStruct((m, SWA_QD), BF16),
        compiler_params=_cparams(("parallel", "arbitrary")),
        name="swa",
    )(sinks, p2, p2, p2, p2, p2)


def _diff_body(q_ref, k_ref, v_ref, lamp_ref, nw_ref, o_ref, *, seq, tq, lambda_init):
    dh = DIFF_HEAD_DIM
    lane = _iota((1, 2 * dh), 1)
    lp = lamp_ref[...]
    lam = (jnp.exp(jnp.sum(lp[0:1] * lp[1:2], axis=-1, keepdims=True))
           - jnp.exp(jnp.sum(lp[2:3] * lp[3:4], axis=-1, keepdims=True)) + lambda_init)
    on_diag = _iota((tq, tq), 1) <= _iota((tq, tq), 0)
    kb = k_ref[...].astype(BF16)
    vb = v_ref[...].astype(BF16)
    for qi in range(seq // tq):
        lo = qi * tq
        q = q_ref[lo:lo + tq, :] * (dh ** -0.5)
        outs = []
        for half in range(2):
            in_half = (lane < dh) if half == 0 else (lane >= dh)
            qh = jnp.where(in_half, q, 0.0).astype(BF16)
            s_d = jnp.where(on_diag, _dot_nt(qh, kb[lo:lo + tq]), NEG)
            mx = jnp.max(s_d, axis=-1, keepdims=True)
            if qi > 0:
                s_o = _dot_nt(qh, kb[:lo])
                mx = jnp.maximum(mx, jnp.max(s_o, axis=-1, keepdims=True))
            p_d = jnp.exp(s_d - mx)
            den = jnp.sum(p_d, axis=-1, keepdims=True)
            o = _dot(p_d.astype(BF16), vb[lo:lo + tq])
            if qi > 0:
                p_o = jnp.exp(s_o - mx)
                den = den + jnp.sum(p_o, axis=-1, keepdims=True)
                o = o + _dot(p_o.astype(BF16), vb[:lo])
            outs.append(o / den)
        o = outs[0] - lam * outs[1]
        o_ref[lo:lo + tq, :] = (_rms(o, nw_ref[...]) * (1.0 - lambda_init)).astype(o_ref.dtype)


def _diff_attn(p2, lam_params, subln_w, lambda_init, *, batch, seq, tq=256):
    m = batch * seq
    hd = 2 * DIFF_HEAD_DIM
    body = functools.partial(_diff_body, seq=seq, tq=tq, lambda_init=lambda_init)
    return pl.pallas_call(
        body,
        grid=(batch, DIFF_HEADS),
        in_specs=[
            pl.BlockSpec((seq, hd), lambda b, h: (b, OFF_QD // hd + h)),
            pl.BlockSpec((seq, hd), lambda b, h: (b, OFF_KD // hd + h)),
            pl.BlockSpec((seq, hd), lambda b, h: (b, OFF_VD // hd + h)),
            pl.BlockSpec((4, DIFF_HEAD_DIM), lambda b, h: (0, 0)),
            pl.BlockSpec((1, hd), lambda b, h: (0, 0)),
        ],
        out_specs=pl.BlockSpec((seq, hd), lambda b, h: (b, h)),
        out_shape=jax.ShapeDtypeStruct((m, DIFF_D), BF16),
        compiler_params=_cparams(("parallel", "parallel")),
        name="diff_attn",
    )(p2, p2, p2, lam_params, subln_w.reshape(1, hd))


R_I1, R_I2, R_G1, R_G2 = 0, 1, 2, 3


def _router_body(h_ref, nw_ref, wr_ref, xn_ref, rec_ref):
    xn = _rms(h_ref[...], nw_ref[...])
    xn_ref[...] = xn
    logits = _dot_x3(xn, wr_ref[...])
    lane = _iota(logits.shape, 1)
    lg = jnp.where(lane < N_EXPERTS, logits, NEG)
    m1 = jnp.max(lg, axis=-1, keepdims=True)
    i1 = jnp.min(jnp.where(lg == m1, lane, LANE), axis=-1, keepdims=True)
    lg2 = jnp.where(lane == i1, NEG, lg)
    m2 = jnp.max(lg2, axis=-1, keepdims=True)
    i2 = jnp.min(jnp.where(lg2 == m2, lane, LANE), axis=-1, keepdims=True)
    e = jnp.exp(m2 - m1)
    g1 = 1.0 / (1.0 + e)
    g2 = e / (1.0 + e)
    rec = jnp.where(lane == R_I1, i1.astype(F32),
                    jnp.where(lane == R_I2, i2.astype(F32),
                              jnp.where(lane == R_G1, g1, jnp.where(lane == R_G2, g2, 0.0))))
    rec_ref[...] = rec


def _router(h, norm_w, w_router, *, tm=512):
    m, d = h.shape
    wr = jnp.zeros((d, LANE), F32).at[:, :N_EXPERTS].set(w_router)
    return pl.pallas_call(
        _router_body,
        grid=(m // tm,),
        in_specs=[
            pl.BlockSpec((tm, d), lambda i: (i, 0)),
            pl.BlockSpec((1, d), lambda i: (0, 0)),
            pl.BlockSpec((d, LANE), lambda i: (0, 0)),
        ],
        out_specs=[pl.BlockSpec((tm, d), lambda i: (i, 0)), pl.BlockSpec((tm, LANE), lambda i: (i, 0))],
        out_shape=[jax.ShapeDtypeStruct((m, d), F32), jax.ShapeDtypeStruct((m, LANE), F32)],
        compiler_params=_cparams(("parallel",)),
        name="router",
    )(h, norm_w.reshape(1, d), wr)


def _row_copy(src, s_row, dst, d_row, sem):
    return pltpu.make_async_copy(src.at[pl.ds(s_row, 1), :], dst.at[pl.ds(d_row, 1), :], sem)


def _scatter_body(dest_ref, x_ref, init_hbm, o_hbm, sem, *, rt):
    del init_hbm

    def issue(r, carry):
        for k in range(TOP_K):
            _row_copy(x_ref, r, o_hbm, dest_ref[0, 0, TOP_K * r + k], sem).start()
        return carry

    lax.fori_loop(0, rt, issue, 0)

    def drain(r, carry):
        for k in range(TOP_K):
            _row_copy(x_ref, 0, o_hbm, 0, sem).wait()
        return carry

    lax.fori_loop(0, rt, drain, 0)


def _scatter_rows(xn, dest, p_rows, *, rt=128):
    m, d = xn.shape
    dest3 = dest.reshape(m // rt, 1, TOP_K * rt)
    init = jnp.zeros((p_rows, d), F32)
    body = functools.partial(_scatter_body, rt=rt)
    return pl.pallas_call(
        body,
        grid=(m // rt,),
        in_specs=[
            pl.BlockSpec((1, 1, TOP_K * rt), lambda i: (i, 0, 0), memory_space=pltpu.SMEM),
            pl.BlockSpec((rt, d), lambda i: (i, 0)),
            pl.BlockSpec(memory_space=pl.ANY),
        ],
        out_specs=pl.BlockSpec(memory_space=pl.ANY),
        out_shape=jax.ShapeDtypeStruct((p_rows, d), F32),
        scratch_shapes=[pltpu.SemaphoreType.DMA(())],
        input_output_aliases={2: 0},
        compiler_params=_cparams(("arbitrary",)),
        name="moe_scatter",
    )(dest3, xn, init)


def _combine_body(dest_ref, y_hbm, h_ref, rec_ref, nw_ref, o_ref, buf_ref, sem, *, rt):
    def issue(r, carry):
        for k in range(TOP_K):
            _row_copy(y_hbm, dest_ref[0, 0, TOP_K * r + k], buf_ref.at[k], r, sem).start()
        return carry

    lax.fori_loop(0, rt, issue, 0)

    def drain(r, carry):
        for k in range(TOP_K):
            _row_copy(y_hbm, 0, buf_ref.at[k], 0, sem).wait()
        return carry

    lax.fori_loop(0, rt, drain, 0)
    rec = rec_ref[...]
    g1 = rec[:, R_G1:R_G1 + 1]
    g2 = rec[:, R_G2:R_G2 + 1]
    hn = h_ref[...] + g1 * buf_ref[0] + g2 * buf_ref[1]
    o_ref[...] = _rms(hn, nw_ref[...])


def _combine(ys, dest, h, rec, final_norm, *, rt=128):
    m, d = h.shape
    dest3 = dest.reshape(m // rt, 1, TOP_K * rt)
    body = functools.partial(_combine_body, rt=rt)
    return pl.pallas_call(
        body,
        grid=(m // rt,),
        in_specs=[
            pl.BlockSpec((1, 1, TOP_K * rt), lambda i: (i, 0, 0), memory_space=pltpu.SMEM),
            pl.BlockSpec(memory_space=pl.ANY),
            pl.BlockSpec((rt, d), lambda i: (i, 0)),
            pl.BlockSpec((rt, LANE), lambda i: (i, 0)),
            pl.BlockSpec((1, d), lambda i: (0, 0)),
        ],
        out_specs=pl.BlockSpec((rt, d), lambda i: (i, 0)),
        out_shape=jax.ShapeDtypeStruct((m, d), F32),
        scratch_shapes=[pltpu.VMEM((TOP_K, rt, d), F32), pltpu.SemaphoreType.DMA(())],
        compiler_params=_cparams(("arbitrary",)),
        name="moe_combine",
    )(dest3, ys, h, rec, final_norm.reshape(1, d))


def _routing_plan(rec, *, tm):
    m = rec.shape[0]
    idx = rec[:, R_I1:R_I2 + 1].astype(jnp.int32)
    flat = idx.reshape(-1)
    onehot = (flat[:, None] == jnp.arange(N_EXPERTS, dtype=jnp.int32)[None, :]).astype(jnp.int32)
    csum = jnp.cumsum(onehot, axis=0)
    rank = jnp.sum(onehot * csum, axis=1) - 1
    counts = csum[-1]
    padded = ((counts + tm - 1) // tm) * tm
    ends = jnp.cumsum(padded)
    starts = ends - padded
    dest = jnp.sum(onehot * starts[None, :], axis=1) + rank
    n_tiles = (TOP_K * m) // tm + N_EXPERTS
    tile_start = jnp.arange(n_tiles, dtype=jnp.int32) * tm
    tile_expert = jnp.minimum(jnp.sum((tile_start[:, None] >= ends[None, :]).astype(jnp.int32), axis=1),
                              N_EXPERTS - 1).astype(jnp.int32)
    n_used = (ends[-1] // tm).astype(jnp.int32).reshape(1)
    return dest.astype(jnp.int32), tile_expert, n_used, n_tiles * tm


def kernel(x, ev_norm_mix, ev_w_in, gla_w_gate_up, gla_b_gate, gla_norm_w, gdn_conv_w, gdn_a_log, gdn_dt_bias,
           gdn_norm_w, ev_w_out, ev_norm_ffn, ffn_w_gate, ffn_w_up, ffn_w_down,
           od_norm_mix, od_w_in, swa_sinks, diff_lambda_q1, diff_lambda_k1, diff_lambda_q2, diff_lambda_k2,
           diff_subln_w, od_w_out, od_norm_ffn, moe_w_router, moe_w_gate, moe_w_up, moe_w_down, final_norm):
    batch, seq, d = x.shape
    m = batch * seq
    h = x.reshape(m, d)

    w_in = ev_w_in[0]
    split = OFF_QB
    gdn_lo = split + GLA_GATE_RANK
    gdn_hi = gdn_lo + 4 * GDN_D
    w_main = jnp.concatenate([w_in[:, :split], w_in[:, gdn_lo:gdn_hi]], axis=1)
    w_misc = jnp.concatenate([w_in[:, split:gdn_lo], w_in[:, gdn_hi:],
                              jnp.zeros((d, LANE - GLA_GATE_RANK - 2 * GDN_HEADS), F32)], axis=1)
    p_main = _matmul((h,), w_main, norm_w=ev_norm_mix[0], out_dtype=F32, tm=1024, tn=512)
    p_misc = _matmul((h,), w_misc, norm_w=ev_norm_mix[0], out_dtype=F32, tm=512, tn=LANE)
    oa = _gla(p_main, p_misc, gla_w_gate_up[0], gla_b_gate[0], gla_norm_w[0], batch=batch, seq=seq)
    ob = _gdn(p_main, p_misc, gdn_conv_w[0], gdn_a_log[0], gdn_dt_bias[0], gdn_norm_w[0], batch=batch, seq=seq)
    h = _matmul((oa, ob), ev_w_out[0], res=h, out_dtype=F32, tm=512, tn=512)
    h = _ffn(h, ev_norm_ffn[0], ffn_w_gate[0], ffn_w_up[0], ffn_w_down[0])

    lambda_init = 0.8 - 0.6 * math.exp(-0.3 * 1)
    p2 = _matmul((h,), od_w_in[0], norm_w=od_norm_mix[0], out_dtype=F32, tm=1024, tn=256)
    oc = _swa(p2, swa_sinks[0], batch=batch, seq=seq)
    lam_params = jnp.stack([diff_lambda_q1[0], diff_lambda_k1[0], diff_lambda_q2[0], diff_lambda_k2[0]])
    od = _diff_attn(p2, lam_params, diff_subln_w[0], lambda_init, batch=batch, seq=seq)
    h = _matmul((oc, od), od_w_out[0], res=h, out_dtype=F32, tm=512, tn=512)

    moe_tm = 512
    xn, rec = _router(h, od_norm_ffn[0], moe_w_router[0])
    dest, tile_expert, n_used, p_rows = _routing_plan(rec, tm=moe_tm)
    xs = _scatter_rows(xn, dest, p_rows)
    ys = _moe_experts(xs, tile_expert, n_used, moe_w_gate[0], moe_w_up[0], moe_w_down[0],
                      tm=moe_tm, tf=512, tn=256)
    out = _combine(ys, dest, h, rec, final_norm)
    return out.reshape(batch, seq, d)
```

```python
import functools
import math

import jax
import jax.numpy as jnp
from jax import lax
from jax.experimental import pallas as pl
from jax.experimental.pallas import tpu as pltpu

F32 = jnp.float32
BF16 = jnp.bfloat16

D_MODEL = 2048
GLA_HEADS = 8
GLA_DK = 64
GLA_DV = 128
GLA_GATE_RANK = 16
GLA_GATE_TEMP = 16.0
GDN_HEADS = 8
GDN_DK = 128
GDN_DV = 128
CONV_WIDTH = 4
CHUNK = 64
SUB = 16
SWA_Q_HEADS = 16
SWA_KV_HEADS = 2
SWA_HEAD_DIM = 64
WINDOW = 128
DIFF_HEADS = 8
DIFF_HEAD_DIM = 64
D_FF = 5632
N_EXPERTS = 8
TOP_K = 2
D_EXPERT = 7168
EPS = 1e-6
NEG = -1e30

LANE = 128
VMEM_LIMIT = 56 * 1024 * 1024

GLA_KD = GLA_HEADS * GLA_DK
GLA_VD = GLA_HEADS * GLA_DV
GDN_D = GDN_HEADS * GDN_DK
EVEN_MAIN = 2 * GLA_KD + 2 * GLA_VD + 4 * GDN_D
OFF_QA, OFF_KA, OFF_VA, OFF_GA = 0, GLA_KD, 2 * GLA_KD, 2 * GLA_KD + GLA_VD
OFF_QB = 2 * GLA_KD + 2 * GLA_VD
OFF_KB, OFF_VB, OFF_ZB = OFF_QB + GDN_D, OFF_QB + 2 * GDN_D, OFF_QB + 3 * GDN_D
MISC_RA, MISC_BB, MISC_AB = 0, GLA_GATE_RANK, GLA_GATE_RANK + GDN_HEADS

SWA_QD = SWA_Q_HEADS * SWA_HEAD_DIM
SWA_KVD = SWA_KV_HEADS * SWA_HEAD_DIM
DIFF_D = DIFF_HEADS * 2 * DIFF_HEAD_DIM
OFF_QC, OFF_KC, OFF_VC = 0, SWA_QD, SWA_QD + SWA_KVD
OFF_QD = SWA_QD + 2 * SWA_KVD
OFF_KD, OFF_VD = OFF_QD + DIFF_D, OFF_QD + 2 * DIFF_D
ODD_IN = OFF_VD + DIFF_D


def _cparams(sem):
    return pltpu.CompilerParams(dimension_semantics=sem, vmem_limit_bytes=VMEM_LIMIT)


def _dot(a, b):
    return jnp.dot(a, b, preferred_element_type=F32)


def _dot_nt(a, b):
    return lax.dot_general(a, b, (((1,), (1,)), ((), ())), preferred_element_type=F32)


def _dot_tn(a, b):
    return lax.dot_general(a, b, (((0,), (0,)), ((), ())), preferred_element_type=F32)


def _split2(x):
    hi = x.astype(BF16)
    lo = (x - hi.astype(F32)).astype(BF16)
    return hi, lo


def _dot_x3(a, b):
    ah, al = _split2(a)
    bh, bl = _split2(b)
    return _dot(ah, bh) + (_dot(ah, bl) + _dot(al, bh))


def _dot_exact_lhs(a_bf16, b):
    bh, bl = _split2(b)
    return _dot(a_bf16, bh) + _dot(a_bf16, bl)


def _sigmoid(x):
    return 1.0 / (1.0 + jnp.exp(-x))


def _silu(x):
    return x * _sigmoid(x)


def _rms(x, w):
    return x * lax.rsqrt(jnp.mean(x * x, axis=-1, keepdims=True) + EPS) * w


def _iota(shape, dim):
    return lax.broadcasted_iota(jnp.int32, shape, dim)


def _mm_body(*refs, n_parts, has_norm, has_res):
    x_refs = refs[:n_parts]
    pos = n_parts
    nw_ref = None
    if has_norm:
        nw_ref = refs[pos]
        pos += 1
    w_refs = refs[pos:pos + n_parts]
    pos += n_parts
    res_ref = None
    if has_res:
        res_ref = refs[pos]
        pos += 1
    o_ref = refs[pos]
    xs_ref = refs[pos + 1]

    @pl.when(pl.program_id(1) == 0)
    def _():
        for p in range(n_parts):
            x = x_refs[p][...].astype(F32)
            if has_norm:
                x = _rms(x, nw_ref[...])
            xs_ref[p] = x.astype(BF16)

    acc = _dot(xs_ref[0], w_refs[0][...].astype(BF16))
    for p in range(1, n_parts):
        acc = acc + _dot(xs_ref[p], w_refs[p][...].astype(BF16))
    if has_res:
        acc = acc + res_ref[...]
    o_ref[...] = acc.astype(o_ref.dtype)


def _matmul(xs, w, *, norm_w=None, res=None, out_dtype=F32, tm=512, tn=512):
    n_parts = len(xs)
    m, kp = xs[0].shape
    n = w.shape[1]
    assert w.shape[0] == kp * n_parts and m % tm == 0 and n % tn == 0
    in_specs = [pl.BlockSpec((tm, kp), lambda i, j: (i, 0)) for _ in range(n_parts)]
    args = list(xs)
    if norm_w is not None:
        assert n_parts == 1
        in_specs.append(pl.BlockSpec((1, kp), lambda i, j: (0, 0)))
        args.append(norm_w.reshape(1, kp))
    for p in range(n_parts):
        in_specs.append(pl.BlockSpec((kp, tn), lambda i, j, p=p: (p, j)))
        args.append(w)
    if res is not None:
        in_specs.append(pl.BlockSpec((tm, tn), lambda i, j: (i, j)))
        args.append(res)
    body = functools.partial(_mm_body, n_parts=n_parts, has_norm=norm_w is not None,
                             has_res=res is not None)
    return pl.pallas_call(
        body,
        grid=(m // tm, n // tn),
        in_specs=in_specs,
        out_specs=pl.BlockSpec((tm, tn), lambda i, j: (i, j)),
        out_shape=jax.ShapeDtypeStruct((m, n), out_dtype),
        scratch_shapes=[pltpu.VMEM((n_parts, tm, kp), BF16)],
        compiler_params=_cparams(("parallel", "arbitrary")),
        name="matmul",
    )(*args)


def _moe_gu_body(te_ref, nu_ref, x_ref, nw_ref, wg_ref, wu_ref, o_ref, xs_ref, *, has_norm):
    used = pl.program_id(0) < nu_ref[0]

    @pl.when(used & (pl.program_id(1) == 0))
    def _():
        x = x_ref[...]
        if has_norm:
            x = _rms(x, nw_ref[...])
        xs_ref[...] = x.astype(BF16)

    @pl.when(used)
    def _():
        xs = xs_ref[...]
        g = _dot(xs, wg_ref[...].astype(BF16))
        u = _dot(xs, wu_ref[...].astype(BF16))
        o_ref[...] = (_silu(g) * u).astype(o_ref.dtype)

    @pl.when(jnp.logical_not(used))
    def _():
        o_ref[...] = jnp.zeros_like(o_ref)


def _moe_dn_body(te_ref, nu_ref, h_ref, wd_ref, *rest, has_res):
    o_ref = rest[-1]
    used = pl.program_id(0) < nu_ref[0]

    @pl.when(used)
    def _():
        acc = _dot(h_ref[...], wd_ref[...].astype(BF16))
        if has_res:
            acc = acc + rest[0][...]
        o_ref[...] = acc

    @pl.when(jnp.logical_not(used))
    def _():
        o_ref[...] = jnp.zeros_like(o_ref)


def _moe_experts(xs, tile_expert, n_used, w_gate, w_up, w_down, *, tm, tf, tn, norm_w=None, res=None):
    p, d = xs.shape
    has_norm = norm_w is not None
    has_res = res is not None
    nw = (norm_w if has_norm else jnp.ones((d,), F32)).reshape(1, d)
    f = w_gate.shape[2]
    n_tiles = p // tm
    nj = f // tf
    nn = d // tn

    def row_map(i, j, te, nu):
        return (jnp.minimum(i, nu[0] - 1), 0)

    def w_idx(i, j, te, nu, last):
        ie = jnp.minimum(i, nu[0] - 1)
        return te[ie], jnp.where(i < nu[0], j, last)

    def gu_map(i, j, te, nu):
        e, je = w_idx(i, j, te, nu, nj - 1)
        return (e, 0, je)

    def dn_map(i, j, te, nu):
        e, je = w_idx(i, j, te, nu, nn - 1)
        return (e, 0, je)

    hidden = pl.pallas_call(
        functools.partial(_moe_gu_body, has_norm=has_norm),
        grid_spec=pltpu.PrefetchScalarGridSpec(
            num_scalar_prefetch=2,
            grid=(n_tiles, nj),
            in_specs=[
                pl.BlockSpec((tm, d), row_map),
                pl.BlockSpec((1, d), lambda i, j, te, nu: (0, 0)),
                pl.BlockSpec((None, d, tf), gu_map),
                pl.BlockSpec((None, d, tf), gu_map),
            ],
            out_specs=pl.BlockSpec((tm, tf), lambda i, j, te, nu: (i, j)),
            scratch_shapes=[pltpu.VMEM((tm, d), BF16)],
        ),
        out_shape=jax.ShapeDtypeStruct((p, f), BF16),
        compiler_params=_cparams(("arbitrary", "arbitrary")),
        name="moe_gate_up",
    )(tile_expert, n_used, xs, nw, w_gate, w_up)
    dn_specs = [pl.BlockSpec((tm, f), row_map), pl.BlockSpec((None, f, tn), dn_map)]
    dn_args = [hidden, w_down]
    if has_res:
        dn_specs.append(pl.BlockSpec((tm, tn), lambda i, j, te, nu: (i, j)))
        dn_args.append(res)
    return pl.pallas_call(
        functools.partial(_moe_dn_body, has_res=has_res),
        grid_spec=pltpu.PrefetchScalarGridSpec(
            num_scalar_prefetch=2,
            grid=(n_tiles, nn),
            in_specs=dn_specs,
            out_specs=pl.BlockSpec((tm, tn), lambda i, j, te, nu: (i, j)),
        ),
        out_shape=jax.ShapeDtypeStruct((p, d), F32),
        compiler_params=_cparams(("arbitrary", "arbitrary")),
        name="moe_down",
    )(tile_expert, n_used, *dn_args)


def _gla_body(q_ref, k_ref, v_ref, g_ref, misc_ref, wg_ref, bg_ref, nw_ref, o_ref, st_ref, *, tb):
    @pl.when(pl.program_id(2) == 0)
    def _():
        st_ref[...] = jnp.zeros_like(st_ref)

    c = CHUNK
    two_dk = 2 * GLA_DK
    ra = misc_ref[:, MISC_RA:MISC_RA + GLA_GATE_RANK]
    z = _dot_x3(ra, wg_ref[...]) + bg_ref[...]
    log_a = (jnp.minimum(z, 0.0) - jnp.log(1.0 + jnp.exp(-jnp.abs(z)))) * (1.0 / GLA_GATE_TEMP)

    row = _iota((c, c), 0)
    col = _iota((c, c), 1)
    tril = (row >= col).astype(BF16)
    rblk, cblk = row >> 4, col >> 4
    lane = _iota((1, two_dk), 1)
    head_mask = [(lane < GLA_DK).astype(F32), (lane >= GLA_DK).astype(F32)]
    st_mask = ((_iota((2 * GLA_DV, two_dk), 0) >= GLA_DV) == (_iota((2 * GLA_DV, two_dk), 1) >= GLA_DK))
    nsub = c // SUB

    for ci in range(tb // c):
        rows = slice(ci * c, (ci + 1) * c)
        q = q_ref[rows, :] * (GLA_DK ** -0.5)
        k = k_ref[rows, :]
        v = v_ref[rows, :]
        b = _dot_exact_lhs(tril, log_a[rows, :])
        b4 = b.reshape(nsub, SUB, two_dk)
        bend4 = jnp.broadcast_to(b4[:, SUB - 1:SUB, :], b4.shape)
        k_hat = k * jnp.exp(bend4.reshape(c, two_dk) - b)
        b_end = b[c - 1:c, :]
        st = st_ref[...]

        lhs_parts = []
        for jb in range(nsub - 1):
            bj = b[jb * SUB + SUB - 1:jb * SUB + SUB, :]
            lhs_parts.append(q * jnp.exp(jnp.minimum(b - bj, 0.0)))
        lhs_all = jnp.concatenate(lhs_parts, axis=0)

        q4 = q.reshape(nsub, SUB, two_dk)
        k4 = k.reshape(nsub, SUB, two_dk)
        diag = [jnp.zeros((c, c), F32), jnp.zeros((c, c), F32)]
        cloc = col - (rblk << 4)
        for jj in range(SUB):
            kj = k4[:, jj:jj + 1, :]
            bj = b4[:, jj:jj + 1, :]
            prod = (q4 * kj * jnp.exp(jnp.minimum(b4 - bj, 0.0))).reshape(c, two_dk)
            for hh in range(2):
                colv = jnp.sum(prod * head_mask[hh], axis=-1, keepdims=True)
                diag[hh] = jnp.where(cloc == jj, colv, diag[hh])

        o_inter = _dot_nt((q * jnp.exp(b)).astype(BF16), st.astype(BF16))
        outs = []
        for hh in range(2):
            off = _dot_nt((lhs_all * head_mask[hh]).astype(BF16), k_hat.astype(BF16))
            s_off = jnp.where(cblk == 0, off[0:c], jnp.where(cblk == 1, off[c:2 * c], off[2 * c:3 * c]))
            scores = jnp.where(rblk > cblk, s_off, jnp.where((rblk == cblk) & (row >= col), diag[hh], 0.0))
            vh = v[:, hh * GLA_DV:(hh + 1) * GLA_DV]
            o = _dot(scores.astype(BF16), vh.astype(BF16)) + o_inter[:, hh * GLA_DV:(hh + 1) * GLA_DV]
            gate = _silu(g_ref[rows, hh * GLA_DV:(hh + 1) * GLA_DV])
            outs.append(_rms(o, nw_ref[...]) * gate)
        o_ref[rows, :] = jnp.concatenate(outs, axis=1).astype(o_ref.dtype)

        k_dec = k * jnp.exp(b_end - b)
        upd = _dot_tn(v.astype(BF16), k_dec.astype(BF16))
        st_ref[...] = st * jnp.exp(b_end) + jnp.where(st_mask, upd, 0.0)


def _gla(p_main, p_misc, w_gate_up, b_gate, norm_w, *, batch, seq, tb=256):
    m = batch * seq
    nt = seq // tb
    hp = GLA_HEADS // 2
    two_dk, two_dv = 2 * GLA_DK, 2 * GLA_DV

    def rows(b, h, t):
        return b * nt + t

    body = functools.partial(_gla_body, tb=tb)
    return pl.pallas_call(
        body,
        grid=(batch, hp, nt),
        in_specs=[
            pl.BlockSpec((tb, two_dk), lambda b, h, t: (rows(b, h, t), OFF_QA // two_dk + h)),
            pl.BlockSpec((tb, two_dk), lambda b, h, t: (rows(b, h, t), OFF_KA // two_dk + h)),
            pl.BlockSpec((tb, two_dv), lambda b, h, t: (rows(b, h, t), OFF_VA // two_dv + h)),
            pl.BlockSpec((tb, two_dv), lambda b, h, t: (rows(b, h, t), OFF_GA // two_dv + h)),
            pl.BlockSpec((tb, LANE), lambda b, h, t: (rows(b, h, t), 0)),
            pl.BlockSpec((GLA_GATE_RANK, two_dk), lambda b, h, t: (0, h)),
            pl.BlockSpec((1, two_dk), lambda b, h, t: (0, h)),
            pl.BlockSpec((1, GLA_DV), lambda b, h, t: (0, 0)),
        ],
        out_specs=pl.BlockSpec((tb, two_dv), lambda b, h, t: (rows(b, h, t), h)),
        out_shape=jax.ShapeDtypeStruct((m, GLA_VD), BF16),
        scratch_shapes=[pltpu.VMEM((two_dv, two_dk), F32)],
        compiler_params=_cparams(("parallel", "parallel", "arbitrary")),
        name="gla",
    )(p_main, p_main, p_main, p_main, p_misc, w_gate_up, b_gate.reshape(1, -1), norm_w.reshape(1, -1))


GDN_GROUP = 4
GDN_ROWS = GDN_GROUP * CHUNK
GDN_PAD = 8


def _gdn_body(q_ref, k_ref, v_ref, z_ref, misc_ref, cw_ref, alog_ref, dtb_ref, nw_ref,
              o_ref, xbuf_ref, st_ref, *, nb):
    c = CHUNK
    hw = GDN_HEADS * GDN_DK

    @pl.when(pl.program_id(1) == 0)
    def _():
        st_ref[...] = jnp.zeros_like(st_ref)
        xbuf_ref[:, 0:GDN_PAD, :] = jnp.zeros((nb, GDN_PAD, 3 * hw), F32)

    def l2n(x):
        return x * lax.rsqrt(jnp.sum(x * x, axis=-1, keepdims=True) + EPS)

    row64 = _iota((c, c), 0)
    col64 = _iota((c, c), 1)
    tril64 = (row64 >= col64).astype(BF16)
    n = GDN_ROWS
    row = _iota((n, n), 0)
    col = _iota((n, n), 1)
    same_head = (row >> 6) == (col >> 6)
    eye = jnp.where(row == col, 1.0, 0.0)
    cw = cw_ref[...]

    for bi in range(nb):
        xbuf_ref[bi, GDN_PAD:GDN_PAD + c, 0:hw] = q_ref[bi]
        xbuf_ref[bi, GDN_PAD:GDN_PAD + c, hw:2 * hw] = k_ref[bi]
        xbuf_ref[bi, GDN_PAD:GDN_PAD + c, 2 * hw:3 * hw] = v_ref[bi]
        conv = jnp.zeros((c, 3 * hw), F32)
        for i in range(CONV_WIDTH):
            start = GDN_PAD - (CONV_WIDTH - 1) + i
            conv = conv + xbuf_ref[bi, start:start + c, :] * cw[i:i + 1, :]
        xbuf_ref[bi, 0:GDN_PAD, :] = xbuf_ref[bi, c:c + GDN_PAD, :]
        conv = _silu(conv)

        misc = misc_ref[bi]
        beta_all = _sigmoid(misc)
        sp_in = misc + dtb_ref[...]
        softplus = jnp.maximum(sp_in, 0.0) + jnp.log(1.0 + jnp.exp(-jnp.abs(sp_in)))
        log_g_all = -jnp.exp(alog_ref[...]) * softplus
        gam_all = _dot_exact_lhs(tril64, log_g_all)

        for grp in range(GDN_HEADS // GDN_GROUP):
            q_parts, k_parts, v_parts, beta_parts, gam_parts, gend_parts = [], [], [], [], [], []
            for h in range(GDN_GROUP):
                head = grp * GDN_GROUP + h
                q_parts.append(l2n(conv[:, head * GDN_DK:(head + 1) * GDN_DK]) * (GDN_DK ** -0.5))
                k_parts.append(l2n(conv[:, hw + head * GDN_DK:hw + (head + 1) * GDN_DK]))
                v_parts.append(conv[:, 2 * hw + head * GDN_DV:2 * hw + (head + 1) * GDN_DV])
                beta_parts.append(beta_all[:, MISC_BB + head:MISC_BB + head + 1])
                gcol = gam_all[:, MISC_AB + head:MISC_AB + head + 1]
                gam_parts.append(gcol)
                gend_parts.append(jnp.broadcast_to(gcol[c - 1:c, :], (c, 1)))
            qst = jnp.concatenate(q_parts, axis=0)
            kst = jnp.concatenate(k_parts, axis=0)
            vst = jnp.concatenate(v_parts, axis=0)
            beta = jnp.concatenate(beta_parts, axis=0)
            gam = jnp.concatenate(gam_parts, axis=0)
            gend = jnp.concatenate(gend_parts, axis=0)

            gam_b = jnp.broadcast_to(gam, (n, n))
            decay = jnp.exp(jnp.minimum(gam_b - gam_b.T, 0.0))
            kb = kst.astype(BF16)
            gram = _dot_nt(kb, kb)
            a_mat = jnp.where(same_head & (row > col), beta * gram * decay, 0.0)

            inv = eye - jnp.where(((row ^ col) == 1) & (row > col), a_mat, 0.0)
            for lg in range(1, 6):
                s = 1 << lg
                sel = ((row >> (lg + 1)) == (col >> (lg + 1))) & ((row & s) != 0) & ((col & s) == 0)
                a_s = jnp.where(sel, a_mat, 0.0)
                inv = inv - _dot_x3(_dot_x3(inv, a_s), inv)

            rk = (beta * jnp.exp(gam)) * kst
            rv = beta * vst
            wu = _dot_x3(inv, jnp.concatenate([rk, rv], axis=1))
            w_mat, u0 = wu[:, :GDN_DK], wu[:, GDN_DK:]
            qk = jnp.where(same_head & (row >= col), _dot_nt(qst.astype(BF16), kb) * decay, 0.0)
            qg = (qst * jnp.exp(gam)).astype(BF16)
            kd = (kst * jnp.exp(gend - gam)).astype(BF16)
            ge = jnp.exp(gend)

            u_parts, oi_parts = [], []
            for h in range(GDN_GROUP):
                slot = bi * GDN_HEADS + grp * GDN_GROUP + h
                r = slice(h * c, (h + 1) * c)
                st = st_ref[slot]
                stb = st.astype(BF16)
                u = u0[r] - _dot_nt(w_mat[r].astype(BF16), stb)
                oi_parts.append(_dot_nt(qg[r], stb))
                st_ref[slot] = st * ge[h * c:h * c + 1, :] + _dot_tn(u.astype(BF16), kd[r])
                u_parts.append(u)
            ust = jnp.concatenate(u_parts, axis=0)
            o = jnp.concatenate(oi_parts, axis=0) + _dot(qk.astype(BF16), ust.astype(BF16))

            for h in range(GDN_GROUP):
                head = grp * GDN_GROUP + h
                lanes = slice(head * GDN_DV, (head + 1) * GDN_DV)
                oh = _rms(o[h * c:(h + 1) * c], nw_ref[...]) * _silu(z_ref[bi, :, lanes])
                o_ref[bi, :, lanes] = oh.astype(o_ref.dtype)


def _gdn(p_main, p_misc, conv_w, a_log, dt_bias, norm_w, *, batch, seq, nb=1):
    nc = seq // CHUNK
    hw = GDN_HEADS * GDN_DK
    assert batch % nb == 0
    pm3 = p_main.reshape(batch, seq, p_main.shape[1])
    misc3 = p_misc.reshape(batch, seq, LANE)

    alog_l = jnp.zeros((1, LANE), F32).at[0, MISC_AB:MISC_AB + GDN_HEADS].set(a_log)
    dtb_l = jnp.zeros((1, LANE), F32).at[0, MISC_AB:MISC_AB + GDN_HEADS].set(dt_bias)
    body = functools.partial(_gdn_body, nb=nb)
    out = pl.pallas_call(
        body,
        grid=(batch // nb, nc),
        in_specs=[
            pl.BlockSpec((nb, CHUNK, hw), lambda b, t: (b, t, OFF_QB // hw)),
            pl.BlockSpec((nb, CHUNK, hw), lambda b, t: (b, t, OFF_KB // hw)),
            pl.BlockSpec((nb, CHUNK, hw), lambda b, t: (b, t, OFF_VB // hw)),
            pl.BlockSpec((nb, CHUNK, hw), lambda b, t: (b, t, OFF_ZB // hw)),
            pl.BlockSpec((nb, CHUNK, LANE), lambda b, t: (b, t, 0)),
            pl.BlockSpec((CONV_WIDTH, 3 * hw), lambda b, t: (0, 0)),
            pl.BlockSpec((1, LANE), lambda b, t: (0, 0)),
            pl.BlockSpec((1, LANE), lambda b, t: (0, 0)),
            pl.BlockSpec((1, GDN_DV), lambda b, t: (0, 0)),
        ],
        out_specs=pl.BlockSpec((nb, CHUNK, hw), lambda b, t: (b, t, 0)),
        out_shape=jax.ShapeDtypeStruct((batch, seq, GDN_D), BF16),
        scratch_shapes=[pltpu.VMEM((nb, CHUNK + GDN_PAD, 3 * hw), F32),
                        pltpu.VMEM((nb * GDN_HEADS, GDN_DV, GDN_DK), F32)],
        compiler_params=_cparams(("parallel", "arbitrary")),
        name="gdn",
    )(pm3, pm3, pm3, pm3, misc3, conv_w, alog_l, dtb_l, norm_w.reshape(1, -1))
    return out.reshape(batch * seq, GDN_D)


def _swa_body(sink_ref, q_ref, kp_ref, kc_ref, vp_ref, vc_ref, o_ref):
    w = WINDOW
    n = pl.program_id(1)
    g = SWA_Q_HEADS // SWA_KV_HEADS
    kk = jnp.concatenate([kp_ref[...], kc_ref[...]], axis=0)
    vv = jnp.concatenate([vp_ref[...], vc_ref[...]], axis=0)
    lane = _iota((1, LANE), 1)
    qpos = _iota((w, 2 * w), 0) + w
    kpos = _iota((w, 2 * w), 1)
    mask = (kpos <= qpos) & (kpos > qpos - w) & ((kpos >= w) | (n > 0))
    scale = SWA_HEAD_DIM ** -0.5
    for hk in range(SWA_KV_HEADS):
        in_head = (lane >= hk * SWA_HEAD_DIM) & (lane < (hk + 1) * SWA_HEAD_DIM)
        k_same = jnp.where(in_head, kk, 0.0)
        v_same = jnp.where(in_head, vv, 0.0)
        k_other = pltpu.roll(k_same, SWA_HEAD_DIM, 1)
        v_other = pltpu.roll(v_same, SWA_HEAD_DIM, 1)
        k_half = [k_same, k_other] if hk == 0 else [k_other, k_same]
        v_half = [v_same, v_other] if hk == 0 else [v_other, v_same]
        for pair in range(g // 2):
            blk = hk * (g // 2) + pair
            q2 = (q_ref[:, blk * LANE:(blk + 1) * LANE] * scale).astype(BF16)
            acc = jnp.zeros((w, LANE), F32)
            for half in range(2):
                head = 2 * blk + half
                s = _dot_nt(q2, k_half[half].astype(BF16))
                s = jnp.where(mask, s, NEG)
                sink = sink_ref[head]
                mx = jnp.maximum(jnp.max(s, axis=-1, keepdims=True), sink)
                p = jnp.exp(s - mx)
                denom = jnp.sum(p, axis=-1, keepdims=True) + jnp.exp(sink - mx)
                p = p / denom
                acc = acc + _dot(p.astype(BF16), v_half[half].astype(BF16))
            o_ref[:, blk * LANE:(blk + 1) * LANE] = acc.astype(o_ref.dtype)


def _swa(p2, sinks, *, batch, seq):
    m = batch * seq
    nb = seq // WINDOW

    def cur(b, n):
        return b * nb + n

    def prev(b, n):
        return b * nb + jnp.maximum(n - 1, 0)

    kcol = OFF_KC // LANE
    vcol = OFF_VC // LANE
    return pl.pallas_call(
        _swa_body,
        grid=(batch, nb),
        in_specs=[
            pl.BlockSpec(memory_space=pltpu.SMEM),
            pl.BlockSpec((WINDOW, SWA_QD), lambda b, n: (cur(b, n), 0)),
            pl.BlockSpec((WINDOW, LANE), lambda b, n: (prev(b, n), kcol)),
            pl.BlockSpec((WINDOW, LANE), lambda b, n: (cur(b, n), kcol)),
            pl.BlockSpec((WINDOW, LANE), lambda b, n: (prev(b, n), vcol)),
            pl.BlockSpec((WINDOW, LANE), lambda b, n: (cur(b, n), vcol)),
        ],
        out_specs=pl.BlockSpec((WINDOW, SWA_QD), lambda b, n: (cur(b, n), 0)),
        out_shape=jax.ShapeDtypeStruct((m, SWA_QD), BF16),
        compiler_params=_cparams(("parallel", "arbitrary")),
        name="swa",
    )(sinks, p2, p2, p2, p2, p2)


def _diff_body(q_ref, k_ref, v_ref, lamp_ref, nw_ref, o_ref, *, seq, tq, lambda_init):
    dh = DIFF_HEAD_DIM
    lane = _iota((1, 2 * dh), 1)
    lp = lamp_ref[...]
    lam = (jnp.exp(jnp.sum(lp[0:1] * lp[1:2], axis=-1, keepdims=True))
           - jnp.exp(jnp.sum(lp[2:3] * lp[3:4], axis=-1, keepdims=True)) + lambda_init)
    on_diag = _iota((tq, tq), 1) <= _iota((tq, tq), 0)
    kb = k_ref[...].astype(BF16)
    vb = v_ref[...].astype(BF16)
    for qi in range(seq // tq):
        lo = qi * tq
        q = q_ref[lo:lo + tq, :] * (dh ** -0.5)
        outs = []
        for half in range(2):
            in_half = (lane < dh) if half == 0 else (lane >= dh)
            qh = jnp.where(in_half, q, 0.0).astype(BF16)
            s_d = jnp.where(on_diag, _dot_nt(qh, kb[lo:lo + tq]), NEG)
            mx = jnp.max(s_d, axis=-1, keepdims=True)
            if qi > 0:
                s_o = _dot_nt(qh, kb[:lo])
                mx = jnp.maximum(mx, jnp.max(s_o, axis=-1, keepdims=True))
            p_d = jnp.exp(s_d - mx)
            den = jnp.sum(p_d, axis=-1, keepdims=True)
            o = _dot(p_d.astype(BF16), vb[lo:lo + tq])
            if qi > 0:
                p_o = jnp.exp(s_o - mx)
                den = den + jnp.sum(p_o, axis=-1, keepdims=True)
                o = o + _dot(p_o.astype(BF16), vb[:lo])
            outs.append(o / den)
        o = outs[0] - lam * outs[1]
        o_ref[lo:lo + tq, :] = (_rms(o, nw_ref[...]) * (1.0 - lambda_init)).astype(o_ref.dtype)


def _diff_attn(p2, lam_params, subln_w, lambda_init, *, batch, seq, tq=256):
    m = batch * seq
    hd = 2 * DIFF_HEAD_DIM
    body = functools.partial(_diff_body, seq=seq, tq=tq, lambda_init=lambda_init)
    return pl.pallas_call(
        body,
        grid=(batch, DIFF_HEADS),
        in_specs=[
            pl.BlockSpec((seq, hd), lambda b, h: (b, OFF_QD // hd + h)),
            pl.BlockSpec((seq, hd), lambda b, h: (b, OFF_KD // hd + h)),
            pl.BlockSpec((seq, hd), lambda b, h: (b, OFF_VD // hd + h)),
            pl.BlockSpec((4, DIFF_HEAD_DIM), lambda b, h: (0, 0)),
            pl.BlockSpec((1, hd), lambda b, h: (0, 0)),
        ],
        out_specs=pl.BlockSpec((seq, hd), lambda b, h: (b, h)),
        out_shape=jax.ShapeDtypeStruct((m, DIFF_D), BF16),
        compiler_params=_cparams(("parallel", "parallel")),
        name="diff_attn",
    )(p2, p2, p2, lam_params, subln_w.reshape(1, hd))


R_I1, R_I2, R_G1, R_G2 = 0, 1, 2, 3


def _router_body(h_ref, nw_ref, wr_ref, xn_ref, rec_ref):
    xn = _rms(h_ref[...], nw_ref[...])
    xn_ref[...] = xn
    logits = _dot_x3(xn, wr_ref[...])
    lane = _iota(logits.shape, 1)
    lg = jnp.where(lane < N_EXPERTS, logits, NEG)
    m1 = jnp.max(lg, axis=-1, keepdims=True)
    i1 = jnp.min(jnp.where(lg == m1, lane, LANE), axis=-1, keepdims=True)
    lg2 = jnp.where(lane == i1, NEG, lg)
    m2 = jnp.max(lg2, axis=-1, keepdims=True)
    i2 = jnp.min(jnp.where(lg2 == m2, lane, LANE), axis=-1, keepdims=True)
    e = jnp.exp(m2 - m1)
    g1 = 1.0 / (1.0 + e)
    g2 = e / (1.0 + e)
    rec = jnp.where(lane == R_I1, i1.astype(F32),
                    jnp.where(lane == R_I2, i2.astype(F32),
                              jnp.where(lane == R_G1, g1, jnp.where(lane == R_G2, g2, 0.0))))
    rec_ref[...] = rec


def _router(h, norm_w, w_router, *, tm=512):
    m, d = h.shape
    wr = jnp.zeros((d, LANE), F32).at[:, :N_EXPERTS].set(w_router)
    return pl.pallas_call(
        _router_body,
        grid=(m // tm,),
        in_specs=[
            pl.BlockSpec((tm, d), lambda i: (i, 0)),
            pl.BlockSpec((1, d), lambda i: (0, 0)),
            pl.BlockSpec((d, LANE), lambda i: (0, 0)),
        ],
        out_specs=[pl.BlockSpec((tm, d), lambda i: (i, 0)), pl.BlockSpec((tm, LANE), lambda i: (i, 0))],
        out_shape=[jax.ShapeDtypeStruct((m, d), F32), jax.ShapeDtypeStruct((m, LANE), F32)],
        compiler_params=_cparams(("parallel",)),
        name="router",
    )(h, norm_w.reshape(1, d), wr)


def _row_copy(src, s_row, dst, d_row, sem):
    return pltpu.make_async_copy(src.at[pl.ds(s_row, 1), :], dst.at[pl.ds(d_row, 1), :], sem)


def _scatter_body(dest_ref, x_ref, init_hbm, o_hbm, sem, *, rt):
    del init_hbm

    def issue(r, carry):
        for k in range(TOP_K):
            _row_copy(x_ref, r, o_hbm, dest_ref[0, 0, TOP_K * r + k], sem).start()
        return carry

    lax.fori_loop(0, rt, issue, 0)

    def drain(r, carry):
        for k in range(TOP_K):
            _row_copy(x_ref, 0, o_hbm, 0, sem).wait()
        return carry

    lax.fori_loop(0, rt, drain, 0)


def _scatter_rows(xn, dest, p_rows, *, rt=128):
    m, d = xn.shape
    dest3 = dest.reshape(m // rt, 1, TOP_K * rt)
    init = jnp.zeros((p_rows, d), F32)
    body = functools.partial(_scatter_body, rt=rt)
    return pl.pallas_call(
        body,
        grid=(m // rt,),
        in_specs=[
            pl.BlockSpec((1, 1, TOP_K * rt), lambda i: (i, 0, 0), memory_space=pltpu.SMEM),
            pl.BlockSpec((rt, d), lambda i: (i, 0)),
            pl.BlockSpec(memory_space=pl.ANY),
        ],
        out_specs=pl.BlockSpec(memory_space=pl.ANY),
        out_shape=jax.ShapeDtypeStruct((p_rows, d), F32),
        scratch_shapes=[pltpu.SemaphoreType.DMA(())],
        input_output_aliases={2: 0},
        compiler_params=_cparams(("arbitrary",)),
        name="moe_scatter",
    )(dest3, xn, init)


def _combine_body(dest_ref, y_hbm, h_ref, rec_ref, nw_ref, o_ref, buf_ref, sem, *, rt):
    def issue(r, carry):
        for k in range(TOP_K):
            _row_copy(y_hbm, dest_ref[0, 0, TOP_K * r + k], buf_ref.at[k], r, sem).start()
        return carry

    lax.fori_loop(0, rt, issue, 0)

    def drain(r, carry):
        for k in range(TOP_K):
            _row_copy(y_hbm, 0, buf_ref.at[k], 0, sem).wait()
        return carry

    lax.fori_loop(0, rt, drain, 0)
    rec = rec_ref[...]
    g1 = rec[:, R_G1:R_G1 + 1]
    g2 = rec[:, R_G2:R_G2 + 1]
    hn = h_ref[...] + g1 * buf_ref[0] + g2 * buf_ref[1]
    o_ref[...] = _rms(hn, nw_ref[...])


def _combine(ys, dest, h, rec, final_norm, *, rt=128):
    m, d = h.shape
    dest3 = dest.reshape(m // rt, 1, TOP_K * rt)
    body = functools.partial(_combine_body, rt=rt)
    return pl.pallas_call(
        body,
        grid=(m // rt,),
        in_specs=[
            pl.BlockSpec((1, 1, TOP_K * rt), lambda i: (i, 0, 0), memory_space=pltpu.SMEM),
            pl.BlockSpec(memory_space=pl.ANY),
            pl.BlockSpec((rt, d), lambda i: (i, 0)),
            pl.BlockSpec((rt, LANE), lambda i: (i, 0)),
            pl.BlockSpec((1, d), lambda i: (0, 0)),
        ],
        out_specs=pl.BlockSpec((rt, d), lambda i: (i, 0)),
        out_shape=jax.ShapeDtypeStruct((m, d), F32),
        scratch_shapes=[pltpu.VMEM((TOP_K, rt, d), F32), pltpu.SemaphoreType.DMA(())],
        compiler_params=_cparams(("arbitrary",)),
        name="moe_combine",
    )(dest3, ys, h, rec, final_norm.reshape(1, d))


def _routing_plan(rec, *, tm):
    m = rec.shape[0]
    idx = rec[:, R_I1:R_I2 + 1].astype(jnp.int32)
    flat = idx.reshape(-1)
    onehot = (flat[:, None] == jnp.arange(N_EXPERTS, dtype=jnp.int32)[None, :]).astype(jnp.int32)
    csum = jnp.cumsum(onehot, axis=0)
    rank = jnp.sum(onehot * csum, axis=1) - 1
    counts = csum[-1]
    padded = ((counts + tm - 1) // tm) * tm
    ends = jnp.cumsum(padded)
    starts = ends - padded
    dest = jnp.sum(onehot * starts[None, :], axis=1) + rank
    n_tiles = (TOP_K * m) // tm + N_EXPERTS
    tile_start = jnp.arange(n_tiles, dtype=jnp.int32) * tm
    tile_expert = jnp.minimum(jnp.sum((tile_start[:, None] >= ends[None, :]).astype(jnp.int32), axis=1),
                              N_EXPERTS - 1).astype(jnp.int32)
    n_used = (ends[-1] // tm).astype(jnp.int32).reshape(1)
    return dest.astype(jnp.int32), tile_expert, n_used, n_tiles * tm


def kernel(x, ev_norm_mix, ev_w_in, gla_w_gate_up, gla_b_gate, gla_norm_w, gdn_conv_w, gdn_a_log, gdn_dt_bias,
           gdn_norm_w, ev_w_out, ev_norm_ffn, ffn_w_gate, ffn_w_up, ffn_w_down,
           od_norm_mix, od_w_in, swa_sinks, diff_lambda_q1, diff_lambda_k1, diff_lambda_q2, diff_lambda_k2,
           diff_subln_w, od_w_out, od_norm_ffn, moe_w_router, moe_w_gate, moe_w_up, moe_w_down, final_norm):
    batch, seq, d = x.shape
    m = batch * seq
    h = x.reshape(m, d)

    w_in = ev_w_in[0]
    split = OFF_QB
    gdn_lo = split + GLA_GATE_RANK
    gdn_hi = gdn_lo + 4 * GDN_D
    w_main = jnp.concatenate([w_in[:, :split], w_in[:, gdn_lo:gdn_hi]], axis=1)
    w_misc = jnp.concatenate([w_in[:, split:gdn_lo], w_in[:, gdn_hi:],
                              jnp.zeros((d, LANE - GLA_GATE_RANK - 2 * GDN_HEADS), F32)], axis=1)
    p_main = _matmul((h,), w_main, norm_w=ev_norm_mix[0], out_dtype=F32, tm=1024, tn=512)
    p_misc = _matmul((h,), w_misc, norm_w=ev_norm_mix[0], out_dtype=F32, tm=512, tn=LANE)
    oa = _gla(p_main, p_misc, gla_w_gate_up[0], gla_b_gate[0], gla_norm_w[0], batch=batch, seq=seq)
    ob = _gdn(p_main, p_misc, gdn_conv_w[0], gdn_a_log[0], gdn_dt_bias[0], gdn_norm_w[0], batch=batch, seq=seq)
    h = _matmul((oa, ob), ev_w_out[0], res=h, out_dtype=F32, tm=1024, tn=512)
    ffn_tm = 1024
    ffn_tiles = m // ffn_tm
    h = _moe_experts(h, jnp.zeros((ffn_tiles,), jnp.int32), jnp.full((1,), ffn_tiles, jnp.int32),
                     ffn_w_gate, ffn_w_up, ffn_w_down, tm=ffn_tm, tf=512, tn=256,
                     norm_w=ev_norm_ffn[0], res=h)

    lambda_init = 0.8 - 0.6 * math.exp(-0.3 * 1)
    p2 = _matmul((h,), od_w_in[0], norm_w=od_norm_mix[0], out_dtype=F32, tm=1024, tn=256)
    oc = _swa(p2, swa_sinks[0], batch=batch, seq=seq)
    lam_params = jnp.stack([diff_lambda_q1[0], diff_lambda_k1[0], diff_lambda_q2[0], diff_lambda_k2[0]])
    od = _diff_attn(p2, lam_params, diff_subln_w[0], lambda_init, batch=batch, seq=seq)
    h = _matmul((oc, od), od_w_out[0], res=h, out_dtype=F32, tm=1024, tn=512)

    moe_tm = 1024
    xn, rec = _router(h, od_norm_ffn[0], moe_w_router[0])
    dest, tile_expert, n_used, p_rows = _routing_plan(rec, tm=moe_tm)
    xs = _scatter_rows(xn, dest, p_rows)
    ys = _moe_experts(xs, tile_expert, n_used, moe_w_gate[0], moe_w_up[0], moe_w_down[0],
                      tm=moe_tm, tf=512, tn=256)
    out = _combine(ys, dest, h, rec, final_norm)
    return out.reshape(batch, seq, d)
```

```python
import functools
import math

import jax
import jax.numpy as jnp
from jax import lax
from jax.experimental import pallas as pl
from jax.experimental.pallas import tpu as pltpu

F32 = jnp.float32
BF16 = jnp.bfloat16

D_MODEL = 2048
GLA_HEADS = 8
GLA_DK = 64
GLA_DV = 128
GLA_GATE_RANK = 16
GLA_GATE_TEMP = 16.0
GDN_HEADS = 8
GDN_DK = 128
GDN_DV = 128
CONV_WIDTH = 4
CHUNK = 64
SUB = 16
SWA_Q_HEADS = 16
SWA_KV_HEADS = 2
SWA_HEAD_DIM = 64
WINDOW = 128
DIFF_HEADS = 8
DIFF_HEAD_DIM = 64
D_FF = 5632
N_EXPERTS = 8
TOP_K = 2
D_EXPERT = 7168
EPS = 1e-6
NEG = -1e30

LANE = 128
VMEM_LIMIT = 56 * 1024 * 1024

GLA_KD = GLA_HEADS * GLA_DK
GLA_VD = GLA_HEADS * GLA_DV
GDN_D = GDN_HEADS * GDN_DK
EVEN_MAIN = 2 * GLA_KD + 2 * GLA_VD + 4 * GDN_D
OFF_QA, OFF_KA, OFF_VA, OFF_GA = 0, GLA_KD, 2 * GLA_KD, 2 * GLA_KD + GLA_VD
OFF_QB = 2 * GLA_KD + 2 * GLA_VD
OFF_KB, OFF_VB, OFF_ZB = OFF_QB + GDN_D, OFF_QB + 2 * GDN_D, OFF_QB + 3 * GDN_D
MISC_RA, MISC_BB, MISC_AB = 0, GLA_GATE_RANK, GLA_GATE_RANK + GDN_HEADS

SWA_QD = SWA_Q_HEADS * SWA_HEAD_DIM
SWA_KVD = SWA_KV_HEADS * SWA_HEAD_DIM
DIFF_D = DIFF_HEADS * 2 * DIFF_HEAD_DIM
OFF_QC, OFF_KC, OFF_VC = 0, SWA_QD, SWA_QD + SWA_KVD
OFF_QD = SWA_QD + 2 * SWA_KVD
OFF_KD, OFF_VD = OFF_QD + DIFF_D, OFF_QD + 2 * DIFF_D
ODD_IN = OFF_VD + DIFF_D


def _cparams(sem):
    return pltpu.CompilerParams(dimension_semantics=sem, vmem_limit_bytes=VMEM_LIMIT)


def _dot(a, b):
    return jnp.dot(a, b, preferred_element_type=F32)


def _dot_nt(a, b):
    return lax.dot_general(a, b, (((1,), (1,)), ((), ())), preferred_element_type=F32)


def _dot_tn(a, b):
    return lax.dot_general(a, b, (((0,), (0,)), ((), ())), preferred_element_type=F32)


def _split2(x):
    hi = x.astype(BF16)
    lo = (x - hi.astype(F32)).astype(BF16)
    return hi, lo


def _dot_x3(a, b):
    ah, al = _split2(a)
    bh, bl = _split2(b)
    return _dot(ah, bh) + (_dot(ah, bl) + _dot(al, bh))


def _dot_exact_lhs(a_bf16, b):
    bh, bl = _split2(b)
    return _dot(a_bf16, bh) + _dot(a_bf16, bl)


def _sigmoid(x):
    return 1.0 / (1.0 + jnp.exp(-x))


def _silu(x):
    return x * _sigmoid(x)


def _rms(x, w):
    return x * lax.rsqrt(jnp.mean(x * x, axis=-1, keepdims=True) + EPS) * w


def _iota(shape, dim):
    return lax.broadcasted_iota(jnp.int32, shape, dim)


def _mm_body(*refs, n_parts, has_norm, has_res):
    x_refs = refs[:n_parts]
    pos = n_parts
    nw_ref = None
    if has_norm:
        nw_ref = refs[pos]
        pos += 1
    w_refs = refs[pos:pos + n_parts]
    pos += n_parts
    res_ref = None
    if has_res:
        res_ref = refs[pos]
        pos += 1
    o_ref = refs[pos]
    xs_ref = refs[pos + 1]

    @pl.when(pl.program_id(1) == 0)
    def _():
        for p in range(n_parts):
            x = x_refs[p][...].astype(F32)
            if has_norm:
                x = _rms(x, nw_ref[...])
            xs_ref[p] = x.astype(BF16)

    acc = _dot(xs_ref[0], w_refs[0][...].astype(BF16))
    for p in range(1, n_parts):
        acc = acc + _dot(xs_ref[p], w_refs[p][...].astype(BF16))
    if has_res:
        acc = acc + res_ref[...]
    o_ref[...] = acc.astype(o_ref.dtype)


def _matmul(xs, w, *, norm_w=None, res=None, out_dtype=F32, tm=512, tn=512):
    n_parts = len(xs)
    m, kp = xs[0].shape
    n = w.shape[1]
    assert w.shape[0] == kp * n_parts and m % tm == 0 and n % tn == 0
    in_specs = [pl.BlockSpec((tm, kp), lambda i, j: (i, 0)) for _ in range(n_parts)]
    args = list(xs)
    if norm_w is not None:
        assert n_parts == 1
        in_specs.append(pl.BlockSpec((1, kp), lambda i, j: (0, 0)))
        args.append(norm_w.reshape(1, kp))
    for p in range(n_parts):
        in_specs.append(pl.BlockSpec((kp, tn), lambda i, j, p=p: (p, j)))
        args.append(w)
    if res is not None:
        in_specs.append(pl.BlockSpec((tm, tn), lambda i, j: (i, j)))
        args.append(res)
    body = functools.partial(_mm_body, n_parts=n_parts, has_norm=norm_w is not None,
                             has_res=res is not None)
    return pl.pallas_call(
        body,
        grid=(m // tm, n // tn),
        in_specs=in_specs,
        out_specs=pl.BlockSpec((tm, tn), lambda i, j: (i, j)),
        out_shape=jax.ShapeDtypeStruct((m, n), out_dtype),
        scratch_shapes=[pltpu.VMEM((n_parts, tm, kp), BF16)],
        compiler_params=_cparams(("parallel", "arbitrary")),
        name="matmul",
    )(*args)


def _moe_gu_body(te_ref, nu_ref, x_ref, nw_ref, wg_ref, wu_ref, o_ref, xs_ref, *, has_norm):
    used = pl.program_id(0) < nu_ref[0]

    @pl.when(used & (pl.program_id(1) == 0))
    def _():
        x = x_ref[...]
        if has_norm:
            x = _rms(x, nw_ref[...])
        xs_ref[...] = x.astype(BF16)

    @pl.when(used)
    def _():
        xs = xs_ref[...]
        g = _dot(xs, wg_ref[...].astype(BF16))
        u = _dot(xs, wu_ref[...].astype(BF16))
        o_ref[...] = (_silu(g) * u).astype(o_ref.dtype)

    @pl.when(jnp.logical_not(used))
    def _():
        o_ref[...] = jnp.zeros_like(o_ref)


def _moe_dn_body(te_ref, nu_ref, h_ref, wd_ref, *rest, has_res):
    o_ref = rest[-1]
    used = pl.program_id(0) < nu_ref[0]

    @pl.when(used)
    def _():
        acc = _dot(h_ref[...], wd_ref[...].astype(BF16))
        if has_res:
            acc = acc + rest[0][...]
        o_ref[...] = acc

    @pl.when(jnp.logical_not(used))
    def _():
        o_ref[...] = jnp.zeros_like(o_ref)


def _moe_experts(xs, tile_expert, n_used, w_gate, w_up, w_down, *, tm, tf, tn, norm_w=None, res=None):
    p, d = xs.shape
    has_norm = norm_w is not None
    has_res = res is not None
    nw = (norm_w if has_norm else jnp.ones((d,), F32)).reshape(1, d)
    f = w_gate.shape[2]
    n_tiles = p // tm
    nj = f // tf
    nn = d // tn

    def row_map(i, j, te, nu):
        return (jnp.minimum(i, nu[0] - 1), 0)

    def w_idx(i, j, te, nu, last):
        ie = jnp.minimum(i, nu[0] - 1)
        return te[ie], jnp.where(i < nu[0], j, last)

    def gu_map(i, j, te, nu):
        e, je = w_idx(i, j, te, nu, nj - 1)
        return (e, 0, je)

    def dn_map(i, j, te, nu):
        e, je = w_idx(i, j, te, nu, nn - 1)
        return (e, 0, je)

    hidden = pl.pallas_call(
        functools.partial(_moe_gu_body, has_norm=has_norm),
        grid_spec=pltpu.PrefetchScalarGridSpec(
            num_scalar_prefetch=2,
            grid=(n_tiles, nj),
            in_specs=[
                pl.BlockSpec((tm, d), row_map),
                pl.BlockSpec((1, d), lambda i, j, te, nu: (0, 0)),
                pl.BlockSpec((None, d, tf), gu_map),
                pl.BlockSpec((None, d, tf), gu_map),
            ],
            out_specs=pl.BlockSpec((tm, tf), lambda i, j, te, nu: (i, j)),
            scratch_shapes=[pltpu.VMEM((tm, d), BF16)],
        ),
        out_shape=jax.ShapeDtypeStruct((p, f), BF16),
        compiler_params=_cparams(("arbitrary", "arbitrary")),
        name="moe_gate_up",
    )(tile_expert, n_used, xs, nw, w_gate, w_up)
    dn_specs = [pl.BlockSpec((tm, f), row_map), pl.BlockSpec((None, f, tn), dn_map)]
    dn_args = [hidden, w_down]
    if has_res:
        dn_specs.append(pl.BlockSpec((tm, tn), lambda i, j, te, nu: (i, j)))
        dn_args.append(res)
    return pl.pallas_call(
        functools.partial(_moe_dn_body, has_res=has_res),
        grid_spec=pltpu.PrefetchScalarGridSpec(
            num_scalar_prefetch=2,
            grid=(n_tiles, nn),
            in_specs=dn_specs,
            out_specs=pl.BlockSpec((tm, tn), lambda i, j, te, nu: (i, j)),
        ),
        out_shape=jax.ShapeDtypeStruct((p, d), F32),
        compiler_params=_cparams(("arbitrary", "arbitrary")),
        name="moe_down",
    )(tile_expert, n_used, *dn_args)


def _gla_body(q_ref, k_ref, v_ref, g_ref, misc_ref, wg_ref, bg_ref, nw_ref, o_ref, st_ref, *, tb):
    @pl.when(pl.program_id(2) == 0)
    def _():
        st_ref[...] = jnp.zeros_like(st_ref)

    c = CHUNK
    two_dk = 2 * GLA_DK
    ra = misc_ref[:, MISC_RA:MISC_RA + GLA_GATE_RANK]
    z = _dot_x3(ra, wg_ref[...]) + bg_ref[...]
    log_a = (jnp.minimum(z, 0.0) - jnp.log(1.0 + jnp.exp(-jnp.abs(z)))) * (1.0 / GLA_GATE_TEMP)

    row = _iota((c, c), 0)
    col = _iota((c, c), 1)
    tril = (row >= col).astype(BF16)
    rblk, cblk = row >> 4, col >> 4
    lane = _iota((1, two_dk), 1)
    head_mask = [(lane < GLA_DK).astype(F32), (lane >= GLA_DK).astype(F32)]
    st_mask = ((_iota((2 * GLA_DV, two_dk), 0) >= GLA_DV) == (_iota((2 * GLA_DV, two_dk), 1) >= GLA_DK))
    nsub = c // SUB

    for ci in range(tb // c):
        rows = slice(ci * c, (ci + 1) * c)
        q = q_ref[rows, :] * (GLA_DK ** -0.5)
        k = k_ref[rows, :]
        v = v_ref[rows, :]
        b = _dot_exact_lhs(tril, log_a[rows, :])
        b4 = b.reshape(nsub, SUB, two_dk)
        bend4 = jnp.broadcast_to(b4[:, SUB - 1:SUB, :], b4.shape)
        k_hat = k * jnp.exp(bend4.reshape(c, two_dk) - b)
        b_end = b[c - 1:c, :]
        st = st_ref[...]

        lhs_parts = []
        for jb in range(nsub - 1):
            bj = b[jb * SUB + SUB - 1:jb * SUB + SUB, :]
            lhs_parts.append(q * jnp.exp(jnp.minimum(b - bj, 0.0)))
        lhs_all = jnp.concatenate(lhs_parts, axis=0)

        q4 = q.reshape(nsub, SUB, two_dk)
        k4 = k.reshape(nsub, SUB, two_dk)
        diag = [jnp.zeros((c, c), F32), jnp.zeros((c, c), F32)]
        cloc = col - (rblk << 4)
        for jj in range(SUB):
            kj = k4[:, jj:jj + 1, :]
            bj = b4[:, jj:jj + 1, :]
            prod = (q4 * kj * jnp.exp(jnp.minimum(b4 - bj, 0.0))).reshape(c, two_dk)
            for hh in range(2):
                colv = jnp.sum(prod * head_mask[hh], axis=-1, keepdims=True)
                diag[hh] = jnp.where(cloc == jj, colv, diag[hh])

        o_inter = _dot_nt((q * jnp.exp(b)).astype(BF16), st.astype(BF16))
        outs = []
        for hh in range(2):
            off = _dot_nt((lhs_all * head_mask[hh]).astype(BF16), k_hat.astype(BF16))
            s_off = jnp.where(cblk == 0, off[0:c], jnp.where(cblk == 1, off[c:2 * c], off[2 * c:3 * c]))
            scores = jnp.where(rblk > cblk, s_off, jnp.where((rblk == cblk) & (row >= col), diag[hh], 0.0))
            vh = v[:, hh * GLA_DV:(hh + 1) * GLA_DV]
            o = _dot(scores.astype(BF16), vh.astype(BF16)) + o_inter[:, hh * GLA_DV:(hh + 1) * GLA_DV]
            gate = _silu(g_ref[rows, hh * GLA_DV:(hh + 1) * GLA_DV])
            outs.append(_rms(o, nw_ref[...]) * gate)
        o_ref[rows, :] = jnp.concatenate(outs, axis=1).astype(o_ref.dtype)

        k_dec = k * jnp.exp(b_end - b)
        upd = _dot_tn(v.astype(BF16), k_dec.astype(BF16))
        st_ref[...] = st * jnp.exp(b_end) + jnp.where(st_mask, upd, 0.0)


def _gla(p_main, p_misc, w_gate_up, b_gate, norm_w, *, batch, seq, tb=256):
    m = batch * seq
    nt = seq // tb
    hp = GLA_HEADS // 2
    two_dk, two_dv = 2 * GLA_DK, 2 * GLA_DV

    def rows(b, h, t):
        return b * nt + t

    body = functools.partial(_gla_body, tb=tb)
    return pl.pallas_call(
        body,
        grid=(batch, hp, nt),
        in_specs=[
            pl.BlockSpec((tb, two_dk), lambda b, h, t: (rows(b, h, t), OFF_QA // two_dk + h)),
            pl.BlockSpec((tb, two_dk), lambda b, h, t: (rows(b, h, t), OFF_KA // two_dk + h)),
            pl.BlockSpec((tb, two_dv), lambda b, h, t: (rows(b, h, t), OFF_VA // two_dv + h)),
            pl.BlockSpec((tb, two_dv), lambda b, h, t: (rows(b, h, t), OFF_GA // two_dv + h)),
            pl.BlockSpec((tb, LANE), lambda b, h, t: (rows(b, h, t), 0)),
            pl.BlockSpec((GLA_GATE_RANK, two_dk), lambda b, h, t: (0, h)),
            pl.BlockSpec((1, two_dk), lambda b, h, t: (0, h)),
            pl.BlockSpec((1, GLA_DV), lambda b, h, t: (0, 0)),
        ],
        out_specs=pl.BlockSpec((tb, two_dv), lambda b, h, t: (rows(b, h, t), h)),
        out_shape=jax.ShapeDtypeStruct((m, GLA_VD), BF16),
        scratch_shapes=[pltpu.VMEM((two_dv, two_dk), F32)],
        compiler_params=_cparams(("parallel", "parallel", "arbitrary")),
        name="gla",
    )(p_main, p_main, p_main, p_main, p_misc, w_gate_up, b_gate.reshape(1, -1), norm_w.reshape(1, -1))


GDN_GROUP = 4
GDN_ROWS = GDN_GROUP * CHUNK
GDN_PAD = 8


def _gdn_body(q_ref, k_ref, v_ref, z_ref, misc_ref, cw_ref, alog_ref, dtb_ref, nw_ref,
              o_ref, xbuf_ref, st_ref, *, nb):
    c = CHUNK
    hw = GDN_HEADS * GDN_DK

    @pl.when(pl.program_id(1) == 0)
    def _():
        st_ref[...] = jnp.zeros_like(st_ref)
        xbuf_ref[:, 0:GDN_PAD, :] = jnp.zeros((nb, GDN_PAD, 3 * hw), F32)

    def l2n(x):
        return x * lax.rsqrt(jnp.sum(x * x, axis=-1, keepdims=True) + EPS)

    row64 = _iota((c, c), 0)
    col64 = _iota((c, c), 1)
    tril64 = (row64 >= col64).astype(BF16)
    n = GDN_ROWS
    row = _iota((n, n), 0)
    col = _iota((n, n), 1)
    same_head = (row >> 6) == (col >> 6)
    eye = jnp.where(row == col, 1.0, 0.0)
    cw = cw_ref[...]

    for bi in range(nb):
        xbuf_ref[bi, GDN_PAD:GDN_PAD + c, 0:hw] = q_ref[bi]
        xbuf_ref[bi, GDN_PAD:GDN_PAD + c, hw:2 * hw] = k_ref[bi]
        xbuf_ref[bi, GDN_PAD:GDN_PAD + c, 2 * hw:3 * hw] = v_ref[bi]
        conv = jnp.zeros((c, 3 * hw), F32)
        for i in range(CONV_WIDTH):
            start = GDN_PAD - (CONV_WIDTH - 1) + i
            conv = conv + xbuf_ref[bi, start:start + c, :] * cw[i:i + 1, :]
        xbuf_ref[bi, 0:GDN_PAD, :] = xbuf_ref[bi, c:c + GDN_PAD, :]
        conv = _silu(conv)

        misc = misc_ref[bi]
        beta_all = _sigmoid(misc)
        sp_in = misc + dtb_ref[...]
        softplus = jnp.maximum(sp_in, 0.0) + jnp.log(1.0 + jnp.exp(-jnp.abs(sp_in)))
        log_g_all = -jnp.exp(alog_ref[...]) * softplus
        gam_all = _dot_exact_lhs(tril64, log_g_all)

        for grp in range(GDN_HEADS // GDN_GROUP):
            q_parts, k_parts, v_parts, beta_parts, gam_parts, gend_parts = [], [], [], [], [], []
            for h in range(GDN_GROUP):
                head = grp * GDN_GROUP + h
                q_parts.append(l2n(conv[:, head * GDN_DK:(head + 1) * GDN_DK]) * (GDN_DK ** -0.5))
                k_parts.append(l2n(conv[:, hw + head * GDN_DK:hw + (head + 1) * GDN_DK]))
                v_parts.append(conv[:, 2 * hw + head * GDN_DV:2 * hw + (head + 1) * GDN_DV])
                beta_parts.append(beta_all[:, MISC_BB + head:MISC_BB + head + 1])
                gcol = gam_all[:, MISC_AB + head:MISC_AB + head + 1]
                gam_parts.append(gcol)
                gend_parts.append(jnp.broadcast_to(gcol[c - 1:c, :], (c, 1)))
            qst = jnp.concatenate(q_parts, axis=0)
            kst = jnp.concatenate(k_parts, axis=0)
            vst = jnp.concatenate(v_parts, axis=0)
            beta = jnp.concatenate(beta_parts, axis=0)
            gam = jnp.concatenate(gam_parts, axis=0)
            gend = jnp.concatenate(gend_parts, axis=0)

            gam_b = jnp.broadcast_to(gam, (n, n))
            decay = jnp.exp(jnp.minimum(gam_b - gam_b.T, 0.0))
            kb = kst.astype(BF16)
            gram = _dot_nt(kb, kb)
            a_mat = jnp.where(same_head & (row > col), beta * gram * decay, 0.0)

            inv = eye - jnp.where(((row ^ col) == 1) & (row > col), a_mat, 0.0)
            for lg in range(1, 6):
                s = 1 << lg
                sel = ((row >> (lg + 1)) == (col >> (lg + 1))) & ((row & s) != 0) & ((col & s) == 0)
                a_s = jnp.where(sel, a_mat, 0.0)
                inv = inv - _dot_x3(_dot_x3(inv, a_s), inv)

            rk = (beta * jnp.exp(gam)) * kst
            rv = beta * vst
            wu = _dot_x3(inv, jnp.concatenate([rk, rv], axis=1))
            w_mat, u0 = wu[:, :GDN_DK], wu[:, GDN_DK:]
            qk = jnp.where(same_head & (row >= col), _dot_nt(qst.astype(BF16), kb) * decay, 0.0)
            qg = (qst * jnp.exp(gam)).astype(BF16)
            kd = (kst * jnp.exp(gend - gam)).astype(BF16)
            ge = jnp.exp(gend)

            u_parts, oi_parts = [], []
            for h in range(GDN_GROUP):
                slot = bi * GDN_HEADS + grp * GDN_GROUP + h
                r = slice(h * c, (h + 1) * c)
                st = st_ref[slot]
                stb = st.astype(BF16)
                u = u0[r] - _dot_nt(w_mat[r].astype(BF16), stb)
                oi_parts.append(_dot_nt(qg[r], stb))
                st_ref[slot] = st * ge[h * c:h * c + 1, :] + _dot_tn(u.astype(BF16), kd[r])
                u_parts.append(u)
            ust = jnp.concatenate(u_parts, axis=0)
            o = jnp.concatenate(oi_parts, axis=0) + _dot(qk.astype(BF16), ust.astype(BF16))

            for h in range(GDN_GROUP):
                head = grp * GDN_GROUP + h
                lanes = slice(head * GDN_DV, (head + 1) * GDN_DV)
                oh = _rms(o[h * c:(h + 1) * c], nw_ref[...]) * _silu(z_ref[bi, :, lanes])
                o_ref[bi, :, lanes] = oh.astype(o_ref.dtype)


def _gdn(p_main, p_misc, conv_w, a_log, dt_bias, norm_w, *, batch, seq, nb=1):
    nc = seq // CHUNK
    hw = GDN_HEADS * GDN_DK
    assert batch % nb == 0
    pm3 = p_main.reshape(batch, seq, p_main.shape[1])
    misc3 = p_misc.reshape(batch, seq, LANE)

    alog_l = jnp.zeros((1, LANE), F32).at[0, MISC_AB:MISC_AB + GDN_HEADS].set(a_log)
    dtb_l = jnp.zeros((1, LANE), F32).at[0, MISC_AB:MISC_AB + GDN_HEADS].set(dt_bias)
    body = functools.partial(_gdn_body, nb=nb)
    out = pl.pallas_call(
        body,
        grid=(batch // nb, nc),
        in_specs=[
            pl.BlockSpec((nb, CHUNK, hw), lambda b, t: (b, t, OFF_QB // hw)),
            pl.BlockSpec((nb, CHUNK, hw), lambda b, t: (b, t, OFF_KB // hw)),
            pl.BlockSpec((nb, CHUNK, hw), lambda b, t: (b, t, OFF_VB // hw)),
            pl.BlockSpec((nb, CHUNK, hw), lambda b, t: (b, t, OFF_ZB // hw)),
            pl.BlockSpec((nb, CHUNK, LANE), lambda b, t: (b, t, 0)),
            pl.BlockSpec((CONV_WIDTH, 3 * hw), lambda b, t: (0, 0)),
            pl.BlockSpec((1, LANE), lambda b, t: (0, 0)),
            pl.BlockSpec((1, LANE), lambda b, t: (0, 0)),
            pl.BlockSpec((1, GDN_DV), lambda b, t: (0, 0)),
        ],
        out_specs=pl.BlockSpec((nb, CHUNK, hw), lambda b, t: (b, t, 0)),
        out_shape=jax.ShapeDtypeStruct((batch, seq, GDN_D), BF16),
        scratch_shapes=[pltpu.VMEM((nb, CHUNK + GDN_PAD, 3 * hw), F32),
                        pltpu.VMEM((nb * GDN_HEADS, GDN_DV, GDN_DK), F32)],
        compiler_params=_cparams(("parallel", "arbitrary")),
        name="gdn",
    )(pm3, pm3, pm3, pm3, misc3, conv_w, alog_l, dtb_l, norm_w.reshape(1, -1))
    return out.reshape(batch * seq, GDN_D)


def _swa_body(sink_ref, q_ref, kp_ref, kc_ref, vp_ref, vc_ref, o_ref):
    w = WINDOW
    n = pl.program_id(1)
    g = SWA_Q_HEADS // SWA_KV_HEADS
    kk = jnp.concatenate([kp_ref[...], kc_ref[...]], axis=0)
    vv = jnp.concatenate([vp_ref[...], vc_ref[...]], axis=0)
    lane = _iota((1, LANE), 1)
    qpos = _iota((w, 2 * w), 0) + w
    kpos = _iota((w, 2 * w), 1)
    mask = (kpos <= qpos) & (kpos > qpos - w) & ((kpos >= w) | (n > 0))
    scale = SWA_HEAD_DIM ** -0.5
    for hk in range(SWA_KV_HEADS):
        in_head = (lane >= hk * SWA_HEAD_DIM) & (lane < (hk + 1) * SWA_HEAD_DIM)
        k_same = jnp.where(in_head, kk, 0.0)
        v_same = jnp.where(in_head, vv, 0.0)
        k_other = pltpu.roll(k_same, SWA_HEAD_DIM, 1)
        v_other = pltpu.roll(v_same, SWA_HEAD_DIM, 1)
        k_half = [k_same, k_other] if hk == 0 else [k_other, k_same]
        v_half = [v_same, v_other] if hk == 0 else [v_other, v_same]
        for pair in range(g // 2):
            blk = hk * (g // 2) + pair
            q2 = (q_ref[:, blk * LANE:(blk + 1) * LANE] * scale).astype(BF16)
            acc = jnp.zeros((w, LANE), F32)
            for half in range(2):
                head = 2 * blk + half
                s = _dot_nt(q2, k_half[half].astype(BF16))
                s = jnp.where(mask, s, NEG)
                sink = sink_ref[head]
                mx = jnp.maximum(jnp.max(s, axis=-1, keepdims=True), sink)
                p = jnp.exp(s - mx)
                denom = jnp.sum(p, axis=-1, keepdims=True) + jnp.exp(sink - mx)
                p = p / denom
                acc = acc + _dot(p.astype(BF16), v_half[half].astype(BF16))
            o_ref[:, blk * LANE:(blk + 1) * LANE] = acc.astype(o_ref.dtype)


def _swa(p2, sinks, *, batch, seq):
    m = batch * seq
    nb = seq // WINDOW

    def cur(b, n):
        return b * nb + n

    def prev(b, n):
        return b * nb + jnp.maximum(n - 1, 0)

    kcol = OFF_KC // LANE
    vcol = OFF_VC // LANE
    return pl.pallas_call(
        _swa_body,
        grid=(batch, nb),
        in_specs=[
            pl.BlockSpec(memory_space=pltpu.SMEM),
            pl.BlockSpec((WINDOW, SWA_QD), lambda b, n: (cur(b, n), 0)),
            pl.BlockSpec((WINDOW, LANE), lambda b, n: (prev(b, n), kcol)),
            pl.BlockSpec((WINDOW, LANE), lambda b, n: (cur(b, n), kcol)),
            pl.BlockSpec((WINDOW, LANE), lambda b, n: (prev(b, n), vcol)),
            pl.BlockSpec((WINDOW, LANE), lambda b, n: (cur(b, n), vcol)),
        ],
        out_specs=pl.BlockSpec((WINDOW, SWA_QD), lambda b, n: (cur(b, n), 0)),
        out_shape=jax.ShapeDtypeStruct((m, SWA_QD), BF16),
        compiler_params=_cparams(("parallel", "arbitrary")),
        name="swa",
    )(sinks, p2, p2, p2, p2, p2)


def _diff_body(q_ref, k_ref, v_ref, lamp_ref, nw_ref, o_ref, *, seq, tq, lambda_init):
    dh = DIFF_HEAD_DIM
    lane = _iota((1, 2 * dh), 1)
    lp = lamp_ref[...]
    lam = (jnp.exp(jnp.sum(lp[0:1] * lp[1:2], axis=-1, keepdims=True))
           - jnp.exp(jnp.sum(lp[2:3] * lp[3:4], axis=-1, keepdims=True)) + lambda_init)
    on_diag = _iota((tq, tq), 1) <= _iota((tq, tq), 0)
    kb = k_ref[...].astype(BF16)
    vb = v_ref[...].astype(BF16)
    for qi in range(seq // tq):
        lo = qi * tq
        q = q_ref[lo:lo + tq, :] * (dh ** -0.5)
        outs = []
        for half in range(2):
            in_half = (lane < dh) if half == 0 else (lane >= dh)
            qh = jnp.where(in_half, q, 0.0).astype(BF16)
            s_d = jnp.where(on_diag, _dot_nt(qh, kb[lo:lo + tq]), NEG)
            mx = jnp.max(s_d, axis=-1, keepdims=True)
            if qi > 0:
                s_o = _dot_nt(qh, kb[:lo])
                mx = jnp.maximum(mx, jnp.max(s_o, axis=-1, keepdims=True))
            p_d = jnp.exp(s_d - mx)
            den = jnp.sum(p_d, axis=-1, keepdims=True)
            o = _dot(p_d.astype(BF16), vb[lo:lo + tq])
            if qi > 0:
                p_o = jnp.exp(s_o - mx)
                den = den + jnp.sum(p_o, axis=-1, keepdims=True)
                o = o + _dot(p_o.astype(BF16), vb[:lo])
            outs.append(o / den)
        o = outs[0] - lam * outs[1]
        o_ref[lo:lo + tq, :] = (_rms(o, nw_ref[...]) * (1.0 - lambda_init)).astype(o_ref.dtype)


def _diff_attn(p2, lam_params, subln_w, lambda_init, *, batch, seq, tq=256):
    m = batch * seq
    hd = 2 * DIFF_HEAD_DIM
    body = functools.partial(_diff_body, seq=seq, tq=tq, lambda_init=lambda_init)
    return pl.pallas_call(
        body,
        grid=(batch, DIFF_HEADS),
        in_specs=[
            pl.BlockSpec((seq, hd), lambda b, h: (b, OFF_QD // hd + h)),
            pl.BlockSpec((seq, hd), lambda b, h: (b, OFF_KD // hd + h)),
            pl.BlockSpec((seq, hd), lambda b, h: (b, OFF_VD // hd + h)),
            pl.BlockSpec((4, DIFF_HEAD_DIM), lambda b, h: (0, 0)),
            pl.BlockSpec((1, hd), lambda b, h: (0, 0)),
        ],
        out_specs=pl.BlockSpec((seq, hd), lambda b, h: (b, h)),
        out_shape=jax.ShapeDtypeStruct((m, DIFF_D), BF16),
        compiler_params=_cparams(("parallel", "parallel")),
        name="diff_attn",
    )(p2, p2, p2, lam_params, subln_w.reshape(1, hd))


R_I1, R_I2, R_G1, R_G2 = 0, 1, 2, 3


def _router_body(h_ref, nw_ref, wr_ref, xn_ref, rec_ref):
    xn = _rms(h_ref[...], nw_ref[...])
    xn_ref[...] = xn
    logits = _dot_x3(xn, wr_ref[...])
    lane = _iota(logits.shape, 1)
    lg = jnp.where(lane < N_EXPERTS, logits, NEG)
    m1 = jnp.max(lg, axis=-1, keepdims=True)
    i1 = jnp.min(jnp.where(lg == m1, lane, LANE), axis=-1, keepdims=True)
    lg2 = jnp.where(lane == i1, NEG, lg)
    m2 = jnp.max(lg2, axis=-1, keepdims=True)
    i2 = jnp.min(jnp.where(lg2 == m2, lane, LANE), axis=-1, keepdims=True)
    e = jnp.exp(m2 - m1)
    g1 = 1.0 / (1.0 + e)
    g2 = e / (1.0 + e)
    rec = jnp.where(lane == R_I1, i1.astype(F32),
                    jnp.where(lane == R_I2, i2.astype(F32),
                              jnp.where(lane == R_G1, g1, jnp.where(lane == R_G2, g2, 0.0))))
    rec_ref[...] = rec


def _router(h, norm_w, w_router, *, tm=512):
    m, d = h.shape
    wr = jnp.zeros((d, LANE), F32).at[:, :N_EXPERTS].set(w_router)
    return pl.pallas_call(
        _router_body,
        grid=(m // tm,),
        in_specs=[
            pl.BlockSpec((tm, d), lambda i: (i, 0)),
            pl.BlockSpec((1, d), lambda i: (0, 0)),
            pl.BlockSpec((d, LANE), lambda i: (0, 0)),
        ],
        out_specs=[pl.BlockSpec((tm, d), lambda i: (i, 0)), pl.BlockSpec((tm, LANE), lambda i: (i, 0))],
        out_shape=[jax.ShapeDtypeStruct((m, d), F32), jax.ShapeDtypeStruct((m, LANE), F32)],
        compiler_params=_cparams(("parallel",)),
        name="router",
    )(h, norm_w.reshape(1, d), wr)


def _row_copy(src, s_row, dst, d_row, sem):
    return pltpu.make_async_copy(src.at[pl.ds(s_row, 1), :], dst.at[pl.ds(d_row, 1), :], sem)


def _scatter_body(dest_ref, x_ref, init_hbm, o_hbm, sem, *, rt):
    del init_hbm

    def issue(r, carry):
        for k in range(TOP_K):
            _row_copy(x_ref, r, o_hbm, dest_ref[0, 0, TOP_K * r + k], sem).start()
        return carry

    lax.fori_loop(0, rt, issue, 0)

    def drain(r, carry):
        for k in range(TOP_K):
            _row_copy(x_ref, 0, o_hbm, 0, sem).wait()
        return carry

    lax.fori_loop(0, rt, drain, 0)


def _scatter_rows(xn, dest, p_rows, *, rt=128):
    m, d = xn.shape
    dest3 = dest.reshape(m // rt, 1, TOP_K * rt)
    init = jnp.zeros((p_rows, d), F32)
    body = functools.partial(_scatter_body, rt=rt)
    return pl.pallas_call(
        body,
        grid=(m // rt,),
        in_specs=[
            pl.BlockSpec((1, 1, TOP_K * rt), lambda i: (i, 0, 0), memory_space=pltpu.SMEM),
            pl.BlockSpec((rt, d), lambda i: (i, 0)),
            pl.BlockSpec(memory_space=pl.ANY),
        ],
        out_specs=pl.BlockSpec(memory_space=pl.ANY),
        out_shape=jax.ShapeDtypeStruct((p_rows, d), F32),
        scratch_shapes=[pltpu.SemaphoreType.DMA(())],
        input_output_aliases={2: 0},
        compiler_params=_cparams(("arbitrary",)),
        name="moe_scatter",
    )(dest3, xn, init)


def _combine_body(dest_ref, dnext_ref, y_hbm, h_ref, rec_ref, nw_ref, o_ref, buf_ref, sems, *, rt):
    i = pl.program_id(0)
    slot = i % 2

    def gather(d_ref, s):
        def issue(r, carry):
            for k in range(TOP_K):
                _row_copy(y_hbm, d_ref[0, 0, TOP_K * r + k], buf_ref.at[s, k], r, sems.at[s]).start()
            return carry

        lax.fori_loop(0, rt, issue, 0)

    @pl.when(i == 0)
    def _():
        gather(dest_ref, 0)

    @pl.when(i + 1 < pl.num_programs(0))
    def _():
        gather(dnext_ref, 1 - slot)

    def drain(r, carry):
        for k in range(TOP_K):
            _row_copy(y_hbm, 0, buf_ref.at[slot, k], 0, sems.at[slot]).wait()
        return carry

    lax.fori_loop(0, rt, drain, 0)
    rec = rec_ref[...]
    g1 = rec[:, R_G1:R_G1 + 1]
    g2 = rec[:, R_G2:R_G2 + 1]
    hn = h_ref[...] + g1 * buf_ref[slot, 0] + g2 * buf_ref[slot, 1]
    o_ref[...] = _rms(hn, nw_ref[...])


def _combine(ys, dest, h, rec, final_norm, *, rt=128):
    m, d = h.shape
    nt = m // rt
    dest3 = dest.reshape(nt, 1, TOP_K * rt)
    body = functools.partial(_combine_body, rt=rt)
    return pl.pallas_call(
        body,
        grid=(nt,),
        in_specs=[
            pl.BlockSpec((1, 1, TOP_K * rt), lambda i: (i, 0, 0), memory_space=pltpu.SMEM),
            pl.BlockSpec((1, 1, TOP_K * rt), lambda i: (jnp.minimum(i + 1, nt - 1), 0, 0),
                         memory_space=pltpu.SMEM),
            pl.BlockSpec(memory_space=pl.ANY),
            pl.BlockSpec((rt, d), lambda i: (i, 0)),
            pl.BlockSpec((rt, LANE), lambda i: (i, 0)),
            pl.BlockSpec((1, d), lambda i: (0, 0)),
        ],
        out_specs=pl.BlockSpec((rt, d), lambda i: (i, 0)),
        out_shape=jax.ShapeDtypeStruct((m, d), F32),
        scratch_shapes=[pltpu.VMEM((2, TOP_K, rt, d), F32), pltpu.SemaphoreType.DMA((2,))],
        compiler_params=_cparams(("arbitrary",)),
        name="moe_combine",
    )(dest3, dest3, ys, h, rec, final_norm.reshape(1, d))


def _routing_plan(rec, *, tm):
    m = rec.shape[0]
    idx = rec[:, R_I1:R_I2 + 1].astype(jnp.int32)
    flat = idx.reshape(-1)
    onehot = (flat[:, None] == jnp.arange(N_EXPERTS, dtype=jnp.int32)[None, :]).astype(jnp.int32)
    csum = jnp.cumsum(onehot, axis=0)
    rank = jnp.sum(onehot * csum, axis=1) - 1
    counts = csum[-1]
    padded = ((counts + tm - 1) // tm) * tm
    ends = jnp.cumsum(padded)
    starts = ends - padded
    dest = jnp.sum(onehot * starts[None, :], axis=1) + rank
    n_tiles = (TOP_K * m) // tm + N_EXPERTS
    tile_start = jnp.arange(n_tiles, dtype=jnp.int32) * tm
    tile_expert = jnp.minimum(jnp.sum((tile_start[:, None] >= ends[None, :]).astype(jnp.int32), axis=1),
                              N_EXPERTS - 1).astype(jnp.int32)
    n_used = (ends[-1] // tm).astype(jnp.int32).reshape(1)
    return dest.astype(jnp.int32), tile_expert, n_used, n_tiles * tm


def kernel(x, ev_norm_mix, ev_w_in, gla_w_gate_up, gla_b_gate, gla_norm_w, gdn_conv_w, gdn_a_log, gdn_dt_bias,
           gdn_norm_w, ev_w_out, ev_norm_ffn, ffn_w_gate, ffn_w_up, ffn_w_down,
           od_norm_mix, od_w_in, swa_sinks, diff_lambda_q1, diff_lambda_k1, diff_lambda_q2, diff_lambda_k2,
           diff_subln_w, od_w_out, od_norm_ffn, moe_w_router, moe_w_gate, moe_w_up, moe_w_down, final_norm):
    batch, seq, d = x.shape
    m = batch * seq
    h = x.reshape(m, d)

    w_in = ev_w_in[0]
    split = OFF_QB
    gdn_lo = split + GLA_GATE_RANK
    gdn_hi = gdn_lo + 4 * GDN_D
    w_main = jnp.concatenate([w_in[:, :split], w_in[:, gdn_lo:gdn_hi]], axis=1)
    w_misc = jnp.concatenate([w_in[:, split:gdn_lo], w_in[:, gdn_hi:],
                              jnp.zeros((d, LANE - GLA_GATE_RANK - 2 * GDN_HEADS), F32)], axis=1)
    p_main = _matmul((h,), w_main, norm_w=ev_norm_mix[0], out_dtype=F32, tm=1024, tn=512)
    p_misc = _matmul((h,), w_misc, norm_w=ev_norm_mix[0], out_dtype=F32, tm=512, tn=LANE)
    oa = _gla(p_main, p_misc, gla_w_gate_up[0], gla_b_gate[0], gla_norm_w[0], batch=batch, seq=seq)
    ob = _gdn(p_main, p_misc, gdn_conv_w[0], gdn_a_log[0], gdn_dt_bias[0], gdn_norm_w[0], batch=batch, seq=seq)
    h = _matmul((oa, ob), ev_w_out[0], res=h, out_dtype=F32, tm=1024, tn=512)
    ffn_tm = 1024
    ffn_tiles = m // ffn_tm
    h = _moe_experts(h, jnp.zeros((ffn_tiles,), jnp.int32), jnp.full((1,), ffn_tiles, jnp.int32),
                     ffn_w_gate, ffn_w_up, ffn_w_down, tm=ffn_tm, tf=512, tn=256,
                     norm_w=ev_norm_ffn[0], res=h)

    lambda_init = 0.8 - 0.6 * math.exp(-0.3 * 1)
    p2 = _matmul((h,), od_w_in[0], norm_w=od_norm_mix[0], out_dtype=F32, tm=1024, tn=256)
    oc = _swa(p2, swa_sinks[0], batch=batch, seq=seq)
    lam_params = jnp.stack([diff_lambda_q1[0], diff_lambda_k1[0], diff_lambda_q2[0], diff_lambda_k2[0]])
    od = _diff_attn(p2, lam_params, diff_subln_w[0], lambda_init, batch=batch, seq=seq)
    h = _matmul((oc, od), od_w_out[0], res=h, out_dtype=F32, tm=1024, tn=512)

    moe_tm = 1024
    xn, rec = _router(h, od_norm_ffn[0], moe_w_router[0])
    dest, tile_expert, n_used, p_rows = _routing_plan(rec, tm=moe_tm)
    xs = _scatter_rows(xn, dest, p_rows)
    ys = _moe_experts(xs, tile_expert, n_used, moe_w_gate[0], moe_w_up[0], moe_w_down[0],
                      tm=moe_tm, tf=512, tn=256)
    out = _combine(ys, dest, h, rec, final_norm)
    return out.reshape(batch, seq, d)
```
